```python
import math
import jax, jax.numpy as jnp
from jax import lax
import numpy as np

D_MODEL = 2048
BATCH = 2
SEQ = 4096
DEPTH = 4
DEC_BATCH = 16
DEC_SEQ = 2048
PAST_LEN = 128

HEAD_DIM = 128
N_HEADS = D_MODEL // HEAD_DIM
GQA_GROUP = 4
N_KV_HEADS = N_HEADS // GQA_GROUP
WINDOW = 128
ATTN_BLOCK = 128
D_SSM = D_MODEL // 2
SSM_GROUP_CH = 16
N_SSM_GROUPS = D_SSM // SSM_GROUP_CH
SSM_STATE = 64
DT_MIN = 1e-3
DT_MAX = 1e-1
N_EXPERTS = 32
TOP_K = 4
D_FF = D_MODEL
SWIGLU_ALPHA = 1.702
SWIGLU_LIMIT = 7.0
MOE_BLOCK = 128
LN_EPS = 1e-5
DEEPNORM_ALPHA = (2.0 * DEPTH) ** 0.25
DEEPNORM_BETA = (8.0 * DEPTH) ** -0.25
Q_W = N_HEADS * HEAD_DIM
KV_W = N_KV_HEADS * HEAD_DIM
IN_W = Q_W + 2 * KV_W + D_SSM + 2 * D_MODEL
SPLITS = (Q_W, Q_W + KV_W, Q_W + 2 * KV_W, Q_W + 2 * KV_W + D_SSM, Q_W + 2 * KV_W + D_SSM + D_MODEL)

kernel_name = 'hybrid_bidir_gqa_s5_moe_encoder'


def layer_norm(x, g, b):
    xf = x.astype(jnp.float32)
    mu = jnp.mean(xf, axis=-1, keepdims=True)
    var = jnp.mean(jnp.square(xf - mu), axis=-1, keepdims=True)
    y = (xf - mu) * lax.rsqrt(var + LN_EPS) * g.astype(jnp.float32) + b.astype(jnp.float32)
    return y.astype(x.dtype)


def alibi_slopes():
    h = jnp.arange(1, N_HEADS + 1, dtype=jnp.float32)
    return jnp.exp2(-8.0 * h / N_HEADS)


def windowed_gqa(q, k, v, sink):
    B_, S_ = q.shape[0], q.shape[1]
    nb = S_ // ATTN_BLOCK
    qb = q.reshape(B_, nb, ATTN_BLOCK, N_KV_HEADS, GQA_GROUP, HEAD_DIM)

    def band(t):
        tp = jnp.pad(t.reshape(B_, nb, ATTN_BLOCK, N_KV_HEADS, HEAD_DIM),
                     ((0, 0), (1, 1), (0, 0), (0, 0), (0, 0)))
        return jnp.concatenate([tp[:, :-2], tp[:, 1:-1], tp[:, 2:]], axis=2)

    kb, vb = band(k), band(v)
    s = jnp.einsum('bnqkgd,bnskd->bnkgqs', qb, kb).astype(jnp.float32) * (HEAD_DIM ** -0.5)
    rel = jnp.arange(ATTN_BLOCK)[:, None] - (jnp.arange(3 * ATTN_BLOCK)[None, :] - ATTN_BLOCK)
    dist = jnp.abs(rel).astype(jnp.float32)
    key_pos = (jnp.arange(nb)[:, None] - 1) * ATTN_BLOCK + jnp.arange(3 * ATTN_BLOCK)[None, :]
    valid = (jnp.abs(rel) <= WINDOW)[None] & ((key_pos >= 0) & (key_pos < S_))[:, None, :]
    bias = -alibi_slopes().reshape(N_KV_HEADS, GQA_GROUP, 1, 1) * dist
    s = jnp.where(valid[None, :, None, None], s + bias, -1e30)
    sink_logit = jnp.broadcast_to(sink.astype(jnp.float32).reshape(N_KV_HEADS, GQA_GROUP, 1, 1),
                                  s.shape[:-1] + (1,))
    p = jax.nn.softmax(jnp.concatenate([s, sink_logit], axis=-1), axis=-1)[..., :-1]
    o = jnp.einsum('bnkgqs,bnskd->bnqkgd', p.astype(v.dtype), vb)
    return o.reshape(B_, S_, N_HEADS * HEAD_DIM)


def _complex_combine(e1, e2):
    a1r, a1i, b1r, b1i = e1
    a2r, a2i, b2r, b2i = e2
    return (a2r * a1r - a2i * a1i,
            a2r * a1i + a2i * a1r,
            a2r * b1r - a2i * b1i + b2r,
            a2r * b1i + a2i * b1r + b2i)


def _diag_scan(lr, li, bur, bui):
    shape = (1,) + bur.shape[1:]
    ar = jnp.broadcast_to(lr, shape)
    ai = jnp.broadcast_to(li, shape)
    _, _, xr, xi = lax.associative_scan(_complex_combine, (ar, ai, bur, bui), axis=1)
    return xr, xi


def s5_bidirectional(u, a_re, a_im, log_dt, b_re, b_im, c_re, c_im, d_skip):
    B_, S_ = u.shape[0], u.shape[1]
    ug = u.reshape(B_, S_, N_SSM_GROUPS, SSM_GROUP_CH).astype(jnp.float32)
    dt = jnp.exp(log_dt.astype(jnp.float32))[..., None]
    ar, ai = a_re.astype(jnp.float32), a_im.astype(jnp.float32)
    mag = jnp.exp(ar * dt)
    lr, li = mag * jnp.cos(ai * dt), mag * jnp.sin(ai * dt)
    den = ar * ar + ai * ai
    zr = ((lr - 1.0) * ar + li * ai) / den
    zi = (li * ar - (lr - 1.0) * ai) / den
    br, bi = b_re.astype(jnp.float32), b_im.astype(jnp.float32)
    bbr = zr[..., None] * br - zi[..., None] * bi
    bbi = zr[..., None] * bi + zi[..., None] * br
    bu_r = jnp.einsum('bsgc,rgpc->rbsgp', ug, bbr)
    bu_i = jnp.einsum('bsgc,rgpc->rbsgp', ug, bbi)
    xf_r, xf_i = _diag_scan(lr[0], li[0], bu_r[0], bu_i[0])
    xb_r, xb_i = _diag_scan(lr[1], li[1], jnp.flip(bu_r[1], axis=1), jnp.flip(bu_i[1], axis=1))
    xr = xf_r + jnp.flip(xb_r, axis=1)
    xi = xf_i + jnp.flip(xb_i, axis=1)
    y = (jnp.einsum('bsgp,gcp->bsgc', xr, c_re.astype(jnp.float32))
         - jnp.einsum('bsgp,gcp->bsgc', xi, c_im.astype(jnp.float32))
         + d_skip.astype(jnp.float32).reshape(N_SSM_GROUPS, SSM_GROUP_CH) * ug)
    return y.reshape(B_, S_, D_SSM).astype(u.dtype)


def moe_ffn(x2d, w_r, b_r, w1, b1, w2, b2):
    T = x2d.shape[0]
    logits = (x2d @ w_r).astype(jnp.float32) + b_r.astype(jnp.float32)
    top_v, top_i = lax.top_k(logits, TOP_K)
    gates = jax.nn.softmax(top_v, axis=-1)
    tk = T * TOP_K
    e_flat = top_i.reshape(-1)
    order = jnp.argsort(e_flat)
    sorted_e = e_flat[order]
    sorted_tok = order // TOP_K
    sorted_w = gates.reshape(-1)[order]
    counts = jnp.bincount(e_flat, length=N_EXPERTS)
    padded = (counts + MOE_BLOCK - 1) // MOE_BLOCK * MOE_BLOCK
    starts = jnp.cumsum(counts) - counts
    pends = jnp.cumsum(padded)
    pstarts = pends - padded
    dest = pstarts[sorted_e] + jnp.arange(tk) - starts[sorted_e]
    n_slots = tk + N_EXPERTS * MOE_BLOCK
    n_blocks = n_slots // MOE_BLOCK
    slot_tok = jnp.full((n_slots,), T, dtype=jnp.int32).at[dest].set(sorted_tok.astype(jnp.int32))
    slot_w = jnp.zeros((n_slots,), jnp.float32).at[dest].set(sorted_w)
    block_e = jnp.clip(jnp.searchsorted(pends, jnp.arange(n_blocks) * MOE_BLOCK, side='right'),
                       0, N_EXPERTS - 1)
    x_pad = jnp.concatenate([x2d, jnp.zeros((1, x2d.shape[1]), x2d.dtype)], axis=0)
    xs = x_pad[slot_tok].reshape(n_blocks, MOE_BLOCK, x2d.shape[1])

    def expert_block(args):
        xb, e = args
        h = xb @ w1[e] + b1[e]
        gate = jnp.minimum(h[:, :D_FF], SWIGLU_LIMIT)
        up = jnp.clip(h[:, D_FF:], -SWIGLU_LIMIT, SWIGLU_LIMIT)
        act = (up + 1.0) * (gate * jax.nn.sigmoid(SWIGLU_ALPHA * gate))
        return act @ w2[e] + b2[e]

    ys = lax.map(expert_block, (xs, block_e)).reshape(n_slots, x2d.shape[1])
    ys = ys * slot_w[:, None].astype(ys.dtype)
    return jax.ops.segment_sum(ys, slot_tok, num_segments=T + 1)[:T]


def encoder_layer(x, w_in, sink, w_ao, a_re, a_im, log_dt, b_re, b_im, c_re, c_im, d_skip,
                  w_glu, w_mix, ln1_g, ln1_b, w_r, b_r, w1, b1, w2, b2, ln2_g, ln2_b):
    B_, S_ = x.shape[0], x.shape[1]
    proj = x @ w_in
    q, k, v, u, g_a, g_b = jnp.split(proj, list(SPLITS), axis=-1)
    attn = windowed_gqa(q.reshape(B_, S_, N_HEADS, HEAD_DIM),
                        k.reshape(B_, S_, N_KV_HEADS, HEAD_DIM),
                        v.reshape(B_, S_, N_KV_HEADS, HEAD_DIM), sink)
    y_a = attn @ w_ao
    y_s = s5_bidirectional(u, a_re, a_im, log_dt, b_re, b_im, c_re, c_im, d_skip)
    h = jax.nn.gelu(y_s) @ w_glu
    y_b = h[..., :D_MODEL] * jax.nn.sigmoid(h[..., D_MODEL:])
    mixed = jax.nn.sigmoid(g_a) * y_a + jax.nn.sigmoid(g_b) * y_b
    x = layer_norm(DEEPNORM_ALPHA * x + mixed @ w_mix, ln1_g, ln1_b)
    f = moe_ffn(x.reshape(B_ * S_, D_MODEL), w_r, b_r, w1, b1, w2, b2).reshape(x.shape)
    return layer_norm(DEEPNORM_ALPHA * x + f, ln2_g, ln2_b)


def setup_inputs(seed: int = 0) -> dict:
    key = jax.random.key(seed)
    ks = jax.random.split(key, 26)
    G, P, GC = N_SSM_GROUPS, SSM_STATE, SSM_GROUP_CH

    def nrm(k, shape, scale):
        return jax.random.normal(k, shape, jnp.float32) * scale

    a_im_base = math.pi * jnp.arange(P, dtype=jnp.float32)
    return {
        'x_prompt': nrm(ks[0], (BATCH, SEQ, D_MODEL), 1.0),
        'x_sample': nrm(ks[1], (DEC_BATCH, DEC_SEQ, D_MODEL), 1.0),
        'w_in': nrm(ks[2], (DEPTH, D_MODEL, IN_W), D_MODEL ** -0.5),
        'attn_sink': nrm(ks[3], (DEPTH, N_HEADS), 0.5),
        'w_attn_out': nrm(ks[4], (DEPTH, Q_W, D_MODEL), Q_W ** -0.5),
        'ssm_a_re': -0.5 + nrm(ks[5], (DEPTH, 2, G, P), 0.01),
        'ssm_a_im': a_im_base + nrm(ks[6], (DEPTH, 2, G, P), 0.01),
        'ssm_log_dt': jax.random.uniform(ks[7], (DEPTH, 2, G), jnp.float32,
                                         math.log(DT_MIN), math.log(DT_MAX)),
        'ssm_b_re': nrm(ks[8], (DEPTH, 2, G, P, GC), (2.0 * GC) ** -0.5),
        'ssm_b_im': nrm(ks[9], (DEPTH, 2, G, P, GC), (2.0 * GC) ** -0.5),
        'ssm_c_re': nrm(ks[10], (DEPTH, G, GC, P), 2.0 ** -0.5),
        'ssm_c_im': nrm(ks[11], (DEPTH, G, GC, P), 2.0 ** -0.5),
        'ssm_d': nrm(ks[12], (DEPTH, D_SSM), 1.0),
        'w_glu': nrm(ks[13], (DEPTH, D_SSM, 2 * D_MODEL), D_SSM ** -0.5),
        'w_mix_out': nrm(ks[14], (DEPTH, D_MODEL, D_MODEL), DEEPNORM_BETA * D_MODEL ** -0.5),
        'ln1_g': 1.0 + nrm(ks[15], (DEPTH, D_MODEL), 0.02),
        'ln1_b': nrm(ks[16], (DEPTH, D_MODEL), 0.02),
        'router_w': nrm(ks[17], (DEPTH, D_MODEL, N_EXPERTS), D_MODEL ** -0.5),
        'router_b': nrm(ks[18], (DEPTH, N_EXPERTS), 0.01),
        'expert_w1': nrm(ks[19], (DEPTH, N_EXPERTS, D_MODEL, 2 * D_FF), D_MODEL ** -0.5),
        'expert_b1': nrm(ks[20], (DEPTH, N_EXPERTS, 2 * D_FF), 0.01),
        'expert_w2': nrm(ks[21], (DEPTH, N_EXPERTS, D_FF, D_MODEL), DEEPNORM_BETA * D_FF ** -0.5),
        'expert_b2': nrm(ks[22], (DEPTH, N_EXPERTS, D_MODEL), 0.01),
        'ln2_g': 1.0 + nrm(ks[23], (DEPTH, D_MODEL), 0.02),
        'ln2_b': nrm(ks[24], (DEPTH, D_MODEL), 0.02),
    }


def reference(x_prompt, x_sample, w_in, attn_sink, w_attn_out, ssm_a_re, ssm_a_im, ssm_log_dt,
              ssm_b_re, ssm_b_im, ssm_c_re, ssm_c_im, ssm_d, w_glu, w_mix_out, ln1_g, ln1_b,
              router_w, router_b, expert_w1, expert_b1, expert_w2, expert_b2, ln2_g, ln2_b):
    def trunk(x):
        for l in range(DEPTH):
            x = encoder_layer(x, w_in[l], attn_sink[l], w_attn_out[l], ssm_a_re[l], ssm_a_im[l],
                              ssm_log_dt[l], ssm_b_re[l], ssm_b_im[l], ssm_c_re[l], ssm_c_im[l],
                              ssm_d[l], w_glu[l], w_mix_out[l], ln1_g[l], ln1_b[l], router_w[l],
                              router_b[l], expert_w1[l], expert_b1[l], expert_w2[l], expert_b2[l],
                              ln2_g[l], ln2_b[l])
        return x

    y_prompt = trunk(x_prompt)
    y_sample = trunk(x_sample)
    return (y_prompt, y_sample)
```

```python
import functools
import math

import jax
import jax.numpy as jnp
from jax import lax
from jax.experimental import pallas as pl
from jax.experimental.pallas import tpu as pltpu

F32 = jnp.float32
BF16 = jnp.bfloat16

LANES = 128
ATTN_BLOCK = 128
SSM_CHUNK = 128
TOP_K = 4
LN_EPS = 1e-5
SWIGLU_ALPHA = 1.702
SWIGLU_LIMIT = 7.0
MASK_VALUE = -1e30
SLOT_BLOCK = 512
TOKEN_TILE = 128
VMEM_LIMIT_BYTES = 56 * 1024 * 1024


def _params(n_grid_dims):
    return pltpu.CompilerParams(dimension_semantics=("arbitrary",) * n_grid_dims,
                                vmem_limit_bytes=VMEM_LIMIT_BYTES)


def _layer_norm_rows(z, g, b):
    mu = jnp.mean(z, axis=-1, keepdims=True)
    zc = z - mu
    var = jnp.mean(zc * zc, axis=-1, keepdims=True)
    return zc * lax.rsqrt(var + LN_EPS) * g + b


def _proj_body(x_ref, w_ref, o_ref, *, gate_tile0):
    acc = jnp.dot(x_ref[...].astype(BF16), w_ref[...], preferred_element_type=F32)
    j = pl.program_id(1)

    @pl.when(j < gate_tile0)
    def _():
        o_ref[...] = acc.astype(o_ref.dtype)

    @pl.when(j >= gate_tile0)
    def _():
        o_ref[...] = jax.nn.sigmoid(acc).astype(o_ref.dtype)


def _in_proj(x, w, gate_col0, tm=1024, tn=512):
    m, k = x.shape
    n = w.shape[1]
    tm, tn = min(tm, m), min(tn, n)
    assert m % tm == 0 and n % tn == 0 and gate_col0 % tn == 0
    return pl.pallas_call(
        functools.partial(_proj_body, gate_tile0=gate_col0 // tn),
        grid=(m // tm, n // tn),
        in_specs=[pl.BlockSpec((tm, k), lambda i, j: (i, 0)),
                  pl.BlockSpec((k, tn), lambda i, j: (0, j))],
        out_specs=pl.BlockSpec((tm, tn), lambda i, j: (i, j)),
        out_shape=jax.ShapeDtypeStruct((m, n), BF16),
        compiler_params=_params(2),
        name="in_proj",
    )(x, w)


def _attn_body(sink_ref, q_ref, kp_ref, kc_ref, kn_ref, vp_ref, vc_ref, vn_ref, bias_ref, o_ref,
               *, n_kv, group, hd, scale, geom):
    i = pl.program_id(0)
    n_blk_a, bps_a, bps_b = geom
    in_a = i < n_blk_a
    pos = jnp.where(in_a, i % bps_a, (i - n_blk_a) % bps_b)
    bps = jnp.where(in_a, bps_a, bps_b)
    first_pen = jnp.where(pos == 0, MASK_VALUE, 0.0).astype(F32)
    last_pen = jnp.where(pos == bps - 1, MASK_VALUE, 0.0).astype(F32)
    lane = lax.broadcasted_iota(jnp.int32, (1, 3 * ATTN_BLOCK), 1)
    pen = jnp.where(lane < ATTN_BLOCK, first_pen, jnp.where(lane >= 2 * ATTN_BLOCK, last_pen, 0.0))
    for g in range(n_kv):
        cs = slice(g * hd, (g + 1) * hd)
        k3 = jnp.concatenate([kp_ref[:, cs], kc_ref[:, cs], kn_ref[:, cs]], axis=0)
        v3 = jnp.concatenate([vp_ref[:, cs], vc_ref[:, cs], vn_ref[:, cs]], axis=0)
        for jj in range(group):
            h = g * group + jj
            qh = q_ref[:, h * hd:(h + 1) * hd]
            s = lax.dot_general(qh, k3, (((1,), (1,)), ((), ())), preferred_element_type=F32)
            s = s * scale + bias_ref[h] + pen
            sink = sink_ref[h]
            m = jnp.maximum(jnp.max(s, axis=-1, keepdims=True), sink)
            p = jnp.exp(s - m)
            den = jnp.sum(p, axis=-1, keepdims=True) + jnp.exp(sink - m)
            o = jnp.dot(p.astype(BF16), v3, preferred_element_type=F32) / den
            o_ref[:, h * hd:(h + 1) * hd] = o.astype(o_ref.dtype)


def _attention(proj, sink, bias, n_heads, n_kv, hd, geom):
    t = proj.shape[0]
    nb = t // ATTN_BLOCK
    q_w, kv_w = n_heads * hd, n_kv * hd
    kblk, vblk = q_w // kv_w, q_w // kv_w + 1
    prev = lambda i: jnp.maximum(i - 1, 0)
    nxt = lambda i: jnp.minimum(i + 1, nb - 1)
    body = functools.partial(_attn_body, n_kv=n_kv, group=n_heads // n_kv, hd=hd,
                             scale=hd ** -0.5, geom=geom)
    return pl.pallas_call(
        body,
        grid=(nb,),
        in_specs=[pl.BlockSpec(memory_space=pltpu.SMEM),
                  pl.BlockSpec((ATTN_BLOCK, q_w), lambda i: (i, 0)),
                  pl.BlockSpec((ATTN_BLOCK, kv_w), lambda i: (prev(i), kblk)),
                  pl.BlockSpec((ATTN_BLOCK, kv_w), lambda i: (i, kblk)),
                  pl.BlockSpec((ATTN_BLOCK, kv_w), lambda i: (nxt(i), kblk)),
                  pl.BlockSpec((ATTN_BLOCK, kv_w), lambda i: (prev(i), vblk)),
                  pl.BlockSpec((ATTN_BLOCK, kv_w), lambda i: (i, vblk)),
                  pl.BlockSpec((ATTN_BLOCK, kv_w), lambda i: (nxt(i), vblk)),
                  pl.BlockSpec((n_heads, ATTN_BLOCK, 3 * ATTN_BLOCK), lambda i: (0, 0, 0))],
        out_specs=pl.BlockSpec((ATTN_BLOCK, q_w), lambda i: (i, 0)),
        out_shape=jax.ShapeDtypeStruct((t, q_w), BF16),
        compiler_params=_params(1),
        name="attention",
    )(sink, proj, proj, proj, proj, proj, proj, proj, bias)


def _attn_bias(n_heads):
    slopes = jnp.exp2(-8.0 * jnp.arange(1, n_heads + 1, dtype=F32) / n_heads)
    rel = jnp.arange(ATTN_BLOCK)[:, None] - (jnp.arange(3 * ATTN_BLOCK)[None, :] - ATTN_BLOCK)
    dist = jnp.abs(rel).astype(F32)
    bias = -slopes[:, None, None] * dist[None]
    return jnp.where((jnp.abs(rel) <= ATTN_BLOCK)[None], bias, MASK_VALUE)


def _ssm_tables(a_re, a_im, log_dt, b_re, b_im, c_re, c_im, d_skip):
    L = SSM_CHUNK
    hp = lax.Precision.HIGHEST
    g_n, p_n, gc = a_re.shape[1], a_re.shape[2], b_re.shape[3]
    dt = jnp.exp(log_dt.astype(F32))[..., None]
    ar, ai = a_re.astype(F32), a_im.astype(F32)
    mag = jnp.exp(ar * dt)
    lr, li = mag * jnp.cos(ai * dt), mag * jnp.sin(ai * dt)
    den = ar * ar + ai * ai
    zr = ((lr - 1.0) * ar + li * ai) / den
    zi = (li * ar - (lr - 1.0) * ai) / den
    br, bi = b_re.astype(F32), b_im.astype(F32)
    bbr = zr[..., None] * br - zi[..., None] * bi
    bbi = zr[..., None] * bi + zi[..., None] * br
    tau = jnp.arange(L + 1, dtype=F32)[:, None, None, None]
    pmag = jnp.exp(tau * (ar * dt)[None])
    ang = tau * (ai * dt)[None]
    pr, pi = pmag * jnp.cos(ang), pmag * jnp.sin(ang)
    cr, ci = c_re.astype(F32), c_im.astype(F32)

    def lag_kernel(r):
        wr = pr[:L, r, :, :, None] * bbr[r][None] - pi[:L, r, :, :, None] * bbi[r][None]
        wi = pr[:L, r, :, :, None] * bbi[r][None] + pi[:L, r, :, :, None] * bbr[r][None]
        return (jnp.einsum('gop,tgpi->giot', cr, wr, precision=hp)
                - jnp.einsum('gop,tgpi->giot', ci, wi, precision=hp))

    kf, kb = lag_kernel(0), lag_kernel(1)
    eye = jnp.eye(gc, dtype=F32)
    center = kf[..., 0] + kb[..., 0] + eye[None] * d_skip.astype(F32).reshape(g_n, 1, gc)
    mvec = jnp.concatenate([jnp.zeros_like(center)[..., None], jnp.flip(kb[..., 1:], axis=-1),
                            center[..., None], kf[..., 1:]], axis=-1)
    mvec = mvec.reshape(g_n, gc * gc, 2 * L)

    pf_r, pf_i = jnp.flip(pr[:L, 0], axis=0), jnp.flip(pi[:L, 0], axis=0)
    pb_r, pb_i = pr[:L, 1], pi[:L, 1]

    def in_map(p_r, p_i, r):
        re = p_r[:, :, :, None] * bbr[r][None] - p_i[:, :, :, None] * bbi[r][None]
        im = p_r[:, :, :, None] * bbi[r][None] + p_i[:, :, :, None] * bbr[r][None]
        return jnp.transpose(re, (1, 3, 0, 2)), jnp.transpose(im, (1, 3, 0, 2))

    fr, fi = in_map(pf_r, pf_i, 0)
    rr, ri = in_map(pb_r, pb_i, 1)
    w_in = jnp.concatenate([fr, rr, fi, ri], axis=-1).reshape(g_n, gc * L, 4 * p_n)

    def out_map(p_r, p_i):
        a_r = cr[None] * p_r[:, :, None, :] - ci[None] * p_i[:, :, None, :]
        a_i = cr[None] * p_i[:, :, None, :] + ci[None] * p_r[:, :, None, :]
        return jnp.transpose(a_r, (1, 3, 2, 0)), jnp.transpose(-a_i, (1, 3, 2, 0))

    of_r, of_i = out_map(pr[1:L + 1, 0], pi[1:L + 1, 0])
    ob_r, ob_i = out_map(jnp.flip(pr[1:L + 1, 1], axis=0), jnp.flip(pi[1:L + 1, 1], axis=0))
    w_out = jnp.concatenate([of_r, ob_r, of_i, ob_i], axis=1).reshape(g_n, 4 * p_n, gc * L)
    lam = jnp.stack([jnp.concatenate([pr[L, 0], pr[L, 1]], axis=-1),
                     jnp.concatenate([pi[L, 0], pi[L, 1]], axis=-1)], axis=1)
    return mvec, w_in.astype(BF16), w_out.astype(BF16), lam


def _ssm_body(u_ref, mvec_ref, win_ref, wout_ref, lam_ref, o_ref, tz_ref, sr_ref, si_ref,
              xr_ref, xi_ref, *, gc, p2, segments):
    L = SSM_CHUNK
    nck = u_ref.shape[1]
    row = lax.broadcasted_iota(jnp.int32, (L, L), 0)
    col = lax.broadcasted_iota(jnp.int32, (L, L), 1)
    upper = col >= row

    def build(i, carry):
        for o in range(gc):
            vec = mvec_ref[0, pl.ds(i * gc + o, 1), :]
            lo = jnp.broadcast_to(vec[:, :L], (L, L))
            hi = jnp.broadcast_to(vec[:, L:], (L, L))
            r_lo = pltpu.roll(lo, 0, 1, stride=1, stride_axis=0)
            r_hi = pltpu.roll(hi, 0, 1, stride=1, stride_axis=0)
            blk = jnp.where(upper, r_hi, r_lo)
            tz_ref[pl.ds(pl.multiple_of(i * L, L), L), o * L:(o + 1) * L] = blk.astype(BF16)
        return carry

    lax.fori_loop(0, gc, build, 0)

    lhs = jnp.concatenate([u_ref[i] for i in range(gc)], axis=-1)
    s = jnp.dot(lhs, win_ref[0], preferred_element_type=F32)
    sr_ref[...] = s[:, 0:p2]
    si_ref[...] = s[:, p2:2 * p2]
    lam_r, lam_i = lam_ref[0, 0:1, :], lam_ref[0, 1:2, :]
    fwd_lane = lax.broadcasted_iota(jnp.int32, (1, p2), 1) < p2 // 2

    for base, nseq, nc in segments:
        cr = jnp.zeros((nseq, p2), F32)
        ci = jnp.zeros((nseq, p2), F32)
        for c in range(nc):
            rows = pl.ds(base + c, nseq, stride=nc) if nseq > 1 else pl.ds(base + c, 1)
            xr_ref[rows, :] = cr
            xi_ref[rows, :] = ci
            sr, si = sr_ref[rows, :], si_ref[rows, :]
            cr, ci = sr + lam_r * cr - lam_i * ci, si + lam_r * ci + lam_i * cr
        cr = jnp.zeros((nseq, p2), F32)
        ci = jnp.zeros((nseq, p2), F32)
        for c in range(nc - 1, -1, -1):
            rows = pl.ds(base + c, nseq, stride=nc) if nseq > 1 else pl.ds(base + c, 1)
            xr_ref[rows, :] = jnp.where(fwd_lane, xr_ref[rows, :], cr)
            xi_ref[rows, :] = jnp.where(fwd_lane, xi_ref[rows, :], ci)
            sr, si = sr_ref[rows, :], si_ref[rows, :]
            cr, ci = sr + lam_r * cr - lam_i * ci, si + lam_r * ci + lam_i * cr

    y = jnp.dot(lhs, tz_ref[...], preferred_element_type=F32)
    x_carry = jnp.concatenate([xr_ref[...], xi_ref[...]], axis=-1).astype(BF16)
    y = y + jnp.dot(x_carry, wout_ref[0], preferred_element_type=F32)
    for o in range(gc):
        o_ref[o] = jax.nn.gelu(y[:, o * L:(o + 1) * L]).astype(o_ref.dtype)


def _ssm(u3, tables, segments):
    mvec, w_in, w_out, lam = tables
    ds, nck, L = u3.shape
    g_n = mvec.shape[0]
    gc = ds // g_n
    p2 = lam.shape[-1]
    body = functools.partial(_ssm_body, gc=gc, p2=p2, segments=segments)
    return pl.pallas_call(
        body,
        grid=(g_n,),
        in_specs=[pl.BlockSpec((gc, nck, L), lambda g: (g, 0, 0)),
                  pl.BlockSpec((1, gc * gc, 2 * L), lambda g: (g, 0, 0)),
                  pl.BlockSpec((1, gc * L, 2 * p2), lambda g: (g, 0, 0)),
                  pl.BlockSpec((1, 2 * p2, gc * L), lambda g: (g, 0, 0)),
                  pl.BlockSpec((1, 2, p2), lambda g: (g, 0, 0))],
        out_specs=pl.BlockSpec((gc, nck, L), lambda g: (g, 0, 0)),
        out_shape=jax.ShapeDtypeStruct((ds, nck, L), BF16),
        scratch_shapes=[pltpu.VMEM((gc * L, gc * L), BF16)] + [pltpu.VMEM((nck, p2), F32)] * 4,
        compiler_params=_params(1),
        name="ssm",
    )(u3, mvec, w_in, w_out, lam)


def _glu_body(x_ref, wa_ref, wb_ref, o_ref):
    x = x_ref[...]
    a = jnp.dot(x, wa_ref[...], preferred_element_type=F32)
    b = jnp.dot(x, wb_ref[...], preferred_element_type=F32)
    o_ref[...] = (a * jax.nn.sigmoid(b)).astype(o_ref.dtype)


def _glu(x, w, tm=1024, tn=512):
    m, k = x.shape
    n = w.shape[1] // 2
    tm, tn = min(tm, m), min(tn, n)
    assert m % tm == 0 and n % tn == 0
    nj = n // tn
    return pl.pallas_call(
        _glu_body,
        grid=(m // tm, nj),
        in_specs=[pl.BlockSpec((tm, k), lambda i, j: (i, 0)),
                  pl.BlockSpec((k, tn), lambda i, j: (0, j)),
                  pl.BlockSpec((k, tn), lambda i, j: (0, j + nj))],
        out_specs=pl.BlockSpec((tm, tn), lambda i, j: (i, j)),
        out_shape=jax.ShapeDtypeStruct((m, n), BF16),
        compiler_params=_params(2),
        name="glu",
    )(x, w, w)


def _mix_body(a_ref, w_ref, ga_ref, gb_ref, yb_ref, o_ref):
    ya = jnp.dot(a_ref[...], w_ref[...], preferred_element_type=F32)
    o_ref[...] = (ga_ref[...].astype(F32) * ya
                  + gb_ref[...].astype(F32) * yb_ref[...].astype(F32)).astype(o_ref.dtype)


def _mix(attn, w_ao, proj, ga_col0, y_b, tm=1024, tn=512):
    m, k = attn.shape
    n = w_ao.shape[1]
    tm, tn = min(tm, m), min(tn, n)
    assert m % tm == 0 and n % tn == 0 and ga_col0 % tn == 0
    ga0, gb0 = ga_col0 // tn, (ga_col0 + n) // tn
    return pl.pallas_call(
        _mix_body,
        grid=(m // tm, n // tn),
        in_specs=[pl.BlockSpec((tm, k), lambda i, j: (i, 0)),
                  pl.BlockSpec((k, tn), lambda i, j: (0, j)),
                  pl.BlockSpec((tm, tn), lambda i, j: (i, ga0 + j)),
                  pl.BlockSpec((tm, tn), lambda i, j: (i, gb0 + j)),
                  pl.BlockSpec((tm, tn), lambda i, j: (i, j))],
        out_specs=pl.BlockSpec((tm, tn), lambda i, j: (i, j)),
        out_shape=jax.ShapeDtypeStruct((m, n), BF16),
        compiler_params=_params(2),
        name="mix",
    )(attn, w_ao, proj, proj, y_b)


def _ln1_router_body(mixed_ref, wmix_ref, x_ref, g_ref, b_ref, wr_ref, br_ref, tri_ref,
                     x1_ref, info_ref, cnt_ref, run_ref, *, alpha):
    i = pl.program_id(0)

    @pl.when(i == 0)
    def _():
        run_ref[...] = jnp.zeros_like(run_ref)

    z = alpha * x_ref[...].astype(F32) + jnp.dot(mixed_ref[...], wmix_ref[...],
                                                  preferred_element_type=F32)
    y = _layer_norm_rows(z, g_ref[...], b_ref[...])
    x1_ref[...] = y
    logits = jnp.dot(y, wr_ref[...], precision=lax.Precision.HIGHEST,
                     preferred_element_type=F32) + br_ref[...]
    tm = logits.shape[0]
    lane = lax.broadcasted_iota(jnp.int32, (tm, LANES), 1).astype(F32)
    rem = logits
    idxs, vals = [], []
    for _ in range(TOP_K):
        mx = jnp.max(rem, axis=-1, keepdims=True)
        idx = jnp.min(jnp.where(rem == mx, lane, float(LANES)), axis=-1, keepdims=True)
        idxs.append(idx)
        vals.append(mx)
        rem = jnp.where(lane == idx, -jnp.inf, rem)
    exps = [jnp.exp(v - vals[0]) for v in vals]
    den = exps[0] + exps[1] + exps[2] + exps[3]
    onehot = jnp.zeros((tm, LANES), F32)
    for idx in idxs:
        onehot = onehot + jnp.where(lane == idx, 1.0, 0.0)
    prefix = jnp.dot(tri_ref[...], onehot.astype(BF16), preferred_element_type=F32) + run_ref[...]
    info = jnp.zeros((tm, LANES), F32)
    for k in range(TOP_K):
        rank = jnp.sum(jnp.where(lane == idxs[k], prefix, 0.0), axis=-1, keepdims=True)
        info = info + jnp.where(lane == float(k), idxs[k], 0.0)
        info = info + jnp.where(lane == float(TOP_K + k), exps[k] / den, 0.0)
        info = info + jnp.where(lane == float(2 * TOP_K + k), rank, 0.0)
    info_ref[...] = info
    run_ref[...] = run_ref[...] + jnp.sum(onehot, axis=0, keepdims=True)
    cnt_ref[...] = run_ref[...]


def _ln1_router(mixed, w_mix, x, ln_g, ln_b, w_r, b_r, alpha, tm=512):
    t, d = mixed.shape
    e = w_r.shape[1]
    tm = min(tm, t)
    assert t % tm == 0 and e <= LANES
    wr_pad = jnp.zeros((d, LANES), F32).at[:, :e].set(w_r.astype(F32))
    br_pad = jnp.full((1, LANES), MASK_VALUE, F32).at[0, :e].set(b_r.astype(F32))
    tri = jnp.tril(jnp.ones((tm, tm), BF16), -1)
    row = lambda v: v.astype(F32).reshape(1, d)
    const = lambda i: (0, 0)
    return pl.pallas_call(
        functools.partial(_ln1_router_body, alpha=alpha),
        grid=(t // tm,),
        in_specs=[pl.BlockSpec((tm, d), lambda i: (i, 0)),
                  pl.BlockSpec((d, d), const),
                  pl.BlockSpec((tm, d), lambda i: (i, 0)),
                  pl.BlockSpec((1, d), const), pl.BlockSpec((1, d), const),
                  pl.BlockSpec((d, LANES), const), pl.BlockSpec((1, LANES), const),
                  pl.BlockSpec((tm, tm), const)],
        out_specs=[pl.BlockSpec((tm, d), lambda i: (i, 0)),
                   pl.BlockSpec((tm, LANES), lambda i: (i, 0)),
                   pl.BlockSpec((1, LANES), const)],
        out_shape=[jax.ShapeDtypeStruct((t, d), F32),
                   jax.ShapeDtypeStruct((t, LANES), F32),
                   jax.ShapeDtypeStruct((1, LANES), F32)],
        scratch_shapes=[pltpu.VMEM((1, LANES), F32)],
        compiler_params=_params(1),
        name="ln1_router",
    )(mixed, w_mix, x, row(ln_g), row(ln_b), wr_pad, br_pad, tri)


def _row_copy(src, src_row, dst, dst_row, sem):
    return pltpu.make_async_copy(src.at[pl.ds(src_row, 1), :], dst.at[pl.ds(dst_row, 1), :], sem)


def _dispatch_body(pos_ref, x_ref, xs_in_hbm, xs_hbm, pos_smem, sem_idx, sem_rows, *, tm):
    del xs_in_hbm
    idx_cp = pltpu.make_async_copy(pos_ref.at[0, 0], pos_smem, sem_idx)
    idx_cp.start()
    idx_cp.wait()

    def issue(r, carry):
        for k in range(TOP_K):
            _row_copy(x_ref, r, xs_hbm, pos_smem[k * tm + r], sem_rows).start()
        return carry

    lax.fori_loop(0, tm, issue, 0)

    def drain(r, carry):
        for k in range(TOP_K):
            _row_copy(x_ref, r, xs_hbm, 0, sem_rows).wait()
        return carry

    lax.fori_loop(0, tm, drain, 0)


def _dispatch(x1, pos_tiles, xs_buf, tm):
    t, d = x1.shape
    return pl.pallas_call(
        functools.partial(_dispatch_body, tm=tm),
        grid=(t // tm,),
        in_specs=[pl.BlockSpec((1, 1, TOP_K * tm), lambda i: (i, 0, 0)),
                  pl.BlockSpec((tm, d), lambda i: (i, 0)),
                  pl.BlockSpec(memory_space=pl.ANY)],
        out_specs=pl.BlockSpec(memory_space=pl.ANY),
        out_shape=jax.ShapeDtypeStruct(xs_buf.shape, F32),
        input_output_aliases={2: 0},
        scratch_shapes=[pltpu.SMEM((TOP_K * tm,), jnp.int32),
                        pltpu.SemaphoreType.DMA(()), pltpu.SemaphoreType.DMA(())],
        compiler_params=_params(1),
        name="moe_dispatch",
    )(pos_tiles, x1, xs_buf)


def _expert_body(be_ref, nv_ref, xs_ref, w1g_ref, w1u_ref, b1g_ref, b1u_ref, w2_ref, b2_ref, o_ref):
    b = pl.program_id(0)
    f = pl.program_id(1)
    nv = nv_ref[b]

    @pl.when(jnp.logical_and(nv == 0, f == 0))
    def _():
        o_ref[...] = jnp.zeros_like(o_ref)

    @pl.when(nv > 0)
    def _():
        tm = xs_ref.shape[0]
        row = lax.broadcasted_iota(jnp.int32, (tm, 1), 0)
        x = jnp.where(row < nv, xs_ref[...], 0.0).astype(BF16)
        hg = jnp.dot(x, w1g_ref[0], preferred_element_type=F32) + b1g_ref[0]
        hu = jnp.dot(x, w1u_ref[0], preferred_element_type=F32) + b1u_ref[0]
        gate = jnp.minimum(hg, SWIGLU_LIMIT)
        up = jnp.clip(hu, -SWIGLU_LIMIT, SWIGLU_LIMIT)
        act = (up + 1.0) * (gate * jax.nn.sigmoid(SWIGLU_ALPHA * gate))
        y = jnp.dot(act.astype(BF16), w2_ref[0], preferred_element_type=F32)

        @pl.when(f == 0)
        def _():
            o_ref[...] = y + b2_ref[0]

        @pl.when(f > 0)
        def _():
            o_ref[...] += y


def _experts(xs, block_e, block_nv, w1, b1, w2, b2, tm, tf=512):
    n_slots, d = xs.shape
    e, ff = w2.shape[0], w2.shape[1]
    tf = min(tf, ff)
    nf = ff // tf
    assert ff % tf == 0 and n_slots % tm == 0
    b1r = b1.astype(F32).reshape(e, 1, 2 * ff)
    b2r = b2.astype(F32).reshape(e, 1, d)
    fi = lambda b, f, nv: jnp.where(nv[b] > 0, f, nf - 1)
    grid_spec = pltpu.PrefetchScalarGridSpec(
        num_scalar_prefetch=2,
        grid=(n_slots // tm, nf),
        in_specs=[pl.BlockSpec((tm, d), lambda b, f, be, nv: (b, 0)),
                  pl.BlockSpec((1, d, tf), lambda b, f, be, nv: (be[b], 0, fi(b, f, nv))),
                  pl.BlockSpec((1, d, tf), lambda b, f, be, nv: (be[b], 0, nf + fi(b, f, nv))),
                  pl.BlockSpec((1, 1, tf), lambda b, f, be, nv: (be[b], 0, fi(b, f, nv))),
                  pl.BlockSpec((1, 1, tf), lambda b, f, be, nv: (be[b], 0, nf + fi(b, f, nv))),
                  pl.BlockSpec((1, tf, d), lambda b, f, be, nv: (be[b], fi(b, f, nv), 0)),
                  pl.BlockSpec((1, 1, d), lambda b, f, be, nv: (be[b], 0, 0))],
        out_specs=pl.BlockSpec((tm, d), lambda b, f, be, nv: (b, 0)),
    )
    return pl.pallas_call(
        _expert_body,
        grid_spec=grid_spec,
        out_shape=jax.ShapeDtypeStruct((n_slots, d), F32),
        compiler_params=_params(2),
        name="moe_experts",
    )(block_e, block_nv, xs, w1, w1, b1r, b1r, w2, b2r)


def _combine_body(pos_ref, info_ref, x1_ref, g_ref, b_ref, ys_hbm, x2_ref, x2b_ref,
                  pos_smem, buf, sem_idx, sem_rows, *, tm, alpha):
    idx_cp = pltpu.make_async_copy(pos_ref.at[0, 0], pos_smem, sem_idx)
    idx_cp.start()
    idx_cp.wait()

    def issue(r, carry):
        for k in range(TOP_K):
            _row_copy(ys_hbm, pos_smem[k * tm + r], buf, k * tm + r, sem_rows).start()
        return carry

    lax.fori_loop(0, tm, issue, 0)

    def drain(r, carry):
        for k in range(TOP_K):
            _row_copy(ys_hbm, 0, buf, k * tm + r, sem_rows).wait()
        return carry

    lax.fori_loop(0, tm, drain, 0)

    info = info_ref[...]
    f = jnp.zeros(x1_ref.shape, F32)
    for k in range(TOP_K):
        f = f + info[:, TOP_K + k:TOP_K + k + 1] * buf[k * tm:(k + 1) * tm, :]
    y = _layer_norm_rows(alpha * x1_ref[...] + f, g_ref[...], b_ref[...])
    x2_ref[...] = y
    x2b_ref[...] = y.astype(BF16)


def _combine_ln2(ys, pos_tiles, info, x1, ln_g, ln_b, alpha, tm):
    t, d = x1.shape
    row = lambda v: v.astype(F32).reshape(1, d)
    const = lambda i: (0, 0)
    return pl.pallas_call(
        functools.partial(_combine_body, tm=tm, alpha=alpha),
        grid=(t // tm,),
        in_specs=[pl.BlockSpec((1, 1, TOP_K * tm), lambda i: (i, 0, 0)),
                  pl.BlockSpec((tm, LANES), lambda i: (i, 0)),
                  pl.BlockSpec((tm, d), lambda i: (i, 0)),
                  pl.BlockSpec((1, d), const), pl.BlockSpec((1, d), const),
                  pl.BlockSpec(memory_space=pl.ANY)],
        out_specs=[pl.BlockSpec((tm, d), lambda i: (i, 0)),
                   pl.BlockSpec((tm, d), lambda i: (i, 0))],
        out_shape=[jax.ShapeDtypeStruct((t, d), F32), jax.ShapeDtypeStruct((t, d), BF16)],
        scratch_shapes=[pltpu.SMEM((TOP_K * tm,), jnp.int32),
                        pltpu.VMEM((TOP_K * tm, d), F32),
                        pltpu.SemaphoreType.DMA(()), pltpu.SemaphoreType.DMA(())],
        compiler_params=_params(1),
        name="moe_combine_ln2",
    )(pos_tiles, info, x1, row(ln_g), row(ln_b), ys)


def _pos_tiles(pos, tm):
    t = pos.shape[0]
    return pos.reshape(t // tm, tm, TOP_K).transpose(0, 2, 1).reshape(t // tm, 1, TOP_K * tm)


def _route(info, counts, n_exp, block):
    t = info.shape[0]
    idx = info[:, 0:TOP_K].astype(jnp.int32)
    rank = info[:, 2 * TOP_K:3 * TOP_K].astype(jnp.int32)
    cnt = counts[0, :n_exp].astype(jnp.int32)
    padded = (cnt + block - 1) // block * block
    pends = jnp.cumsum(padded)
    pstarts = pends - padded
    pos = jnp.take(pstarts, idx, axis=0) + rank
    n_blocks = _num_slot_blocks(t, n_exp, block)
    bstart = jnp.arange(n_blocks, dtype=jnp.int32) * block
    block_e = jnp.clip(jnp.searchsorted(pends, bstart, side='right'), 0, n_exp - 1).astype(jnp.int32)
    block_nv = jnp.clip(jnp.take(pstarts + cnt, block_e) - bstart, 0, block).astype(jnp.int32)
    return pos, block_e, block_nv


def _num_slot_blocks(t, n_exp, block):
    return (t * TOP_K) // block + n_exp


def _encoder_layer(x, xb, xs_buf, geom, w_in, sink, w_ao, ssm_tabs, w_glu, w_mix, ln1_g, ln1_b,
                   w_r, b_r, w1, b1, w2, b2, ln2_g, ln2_b, *, alpha, dims, slot_block, tok_tile):
    n_heads, n_kv, hd, d_ssm, d = dims
    t = x.shape[0]
    q_w, kv_w = n_heads * hd, n_kv * hd
    u_col0 = q_w + 2 * kv_w
    gate_col0 = u_col0 + d_ssm
    attn_geom, ssm_segments = geom

    proj = _in_proj(xb, w_in, gate_col0)
    attn = _attention(proj, sink, _attn_bias(n_heads), n_heads, n_kv, hd, attn_geom)
    u3 = proj[:, u_col0:gate_col0].T.reshape(d_ssm, t // SSM_CHUNK, SSM_CHUNK)
    gy = _ssm(u3, ssm_tabs, ssm_segments).reshape(d_ssm, t).T
    y_b = _glu(gy, w_glu)
    mixed = _mix(attn, w_ao, proj, gate_col0, y_b)
    x1, info, counts = _ln1_router(mixed, w_mix, x, ln1_g, ln1_b, w_r, b_r, alpha)

    pos, block_e, block_nv = _route(info, counts, w_r.shape[1], slot_block)
    pos_t = _pos_tiles(pos, tok_tile)
    xs = _dispatch(x1, pos_t, xs_buf, tok_tile)
    ys = _experts(xs, block_e, block_nv, w1, b1, w2, b2, slot_block)
    x2, x2b = _combine_ln2(ys, pos_t, info, x1, ln2_g, ln2_b, alpha, tok_tile)
    return x2, x2b, xs


def kernel(x_prompt, x_sample, w_in, attn_sink, w_attn_out, ssm_a_re, ssm_a_im, ssm_log_dt, ssm_b_re, ssm_b_im, ssm_c_re, ssm_c_im, ssm_d, w_glu, w_mix_out, ln1_g, ln1_b, router_w, router_b, expert_w1, expert_b1, expert_w2, expert_b2, ln2_g, ln2_b):
    depth = w_in.shape[0]
    bp, sp, d = x_prompt.shape
    bs, ss, _ = x_sample.shape
    n_heads = attn_sink.shape[1]
    hd = d // n_heads
    d_ssm = ssm_a_re.shape[2] * ssm_b_re.shape[4]
    kv_w = (w_in.shape[2] - d - d_ssm - 2 * d) // 2
    dims = (n_heads, kv_w // hd, hd, d_ssm, d)
    alpha = (2.0 * depth) ** 0.25
    assert sp % ATTN_BLOCK == 0 and ss % ATTN_BLOCK == 0 and ATTN_BLOCK == SSM_CHUNK

    tp = bp * sp
    attn_geom = (tp // ATTN_BLOCK, sp // ATTN_BLOCK, ss // ATTN_BLOCK)
    ssm_segments = ((0, bp, sp // SSM_CHUNK), (tp // SSM_CHUNK, bs, ss // SSM_CHUNK))
    geom = (attn_geom, ssm_segments)

    x = jnp.concatenate([x_prompt.reshape(tp, d), x_sample.reshape(bs * ss, d)], axis=0)
    xb = x
    t = x.shape[0]
    slot_block, tok_tile = min(SLOT_BLOCK, t), min(TOKEN_TILE, t)
    n_slots = _num_slot_blocks(t, router_w.shape[2], slot_block) * slot_block
    xs_buf = jnp.zeros((n_slots, d), F32)
    for l in range(depth):
        tabs = _ssm_tables(ssm_a_re[l], ssm_a_im[l], ssm_log_dt[l], ssm_b_re[l], ssm_b_im[l],
                           ssm_c_re[l], ssm_c_im[l], ssm_d[l])
        x, xb, xs_buf = _encoder_layer(
            x, xb, xs_buf, geom, w_in[l].astype(BF16), attn_sink[l].astype(F32),
            w_attn_out[l].astype(BF16), tabs, w_glu[l].astype(BF16), w_mix_out[l].astype(BF16),
            ln1_g[l], ln1_b[l], router_w[l], router_b[l], expert_w1[l].astype(BF16), expert_b1[l],
            expert_w2[l].astype(BF16), expert_b2[l], ln2_g[l], ln2_b[l], alpha=alpha, dims=dims,
            slot_block=slot_block, tok_tile=tok_tile)
    return (x[:tp].reshape(bp, sp, d), x[tp:].reshape(bs, ss, d))
```

```python
import functools
import math

import jax
import jax.numpy as jnp
from jax import lax
from jax.experimental import pallas as pl
from jax.experimental.pallas import tpu as pltpu

F32 = jnp.float32
BF16 = jnp.bfloat16

LANES = 128
ATTN_BLOCK = 128
SSM_CHUNK = 128
TOP_K = 4
LN_EPS = 1e-5
SWIGLU_ALPHA = 1.702
SWIGLU_LIMIT = 7.0
MASK_VALUE = -1e30
LOG2E = math.log2(math.e)
SLOT_BLOCK = 512
TOKEN_TILE = 128
VMEM_LIMIT_BYTES = 56 * 1024 * 1024


def _params(n_grid_dims):
    return pltpu.CompilerParams(dimension_semantics=("arbitrary",) * n_grid_dims,
                                vmem_limit_bytes=VMEM_LIMIT_BYTES)


def _layer_norm_rows(z, g, b):
    mu = jnp.mean(z, axis=-1, keepdims=True)
    zc = z - mu
    var = jnp.mean(zc * zc, axis=-1, keepdims=True)
    return zc * lax.rsqrt(var + LN_EPS) * g + b


def _proj_body(x_ref, w_ref, o_ref, *, gate_tile0):
    acc = jnp.dot(x_ref[...].astype(BF16), w_ref[...], preferred_element_type=F32)
    j = pl.program_id(1)

    @pl.when(j < gate_tile0)
    def _():
        o_ref[...] = acc.astype(o_ref.dtype)

    @pl.when(j >= gate_tile0)
    def _():
        o_ref[...] = jax.nn.sigmoid(acc).astype(o_ref.dtype)


def _in_proj(x, w, gate_col0, tm=1024, tn=512):
    m, k = x.shape
    n = w.shape[1]
    tm, tn = min(tm, m), min(tn, n)
    assert m % tm == 0 and n % tn == 0 and gate_col0 % tn == 0
    return pl.pallas_call(
        functools.partial(_proj_body, gate_tile0=gate_col0 // tn),
        grid=(m // tm, n // tn),
        in_specs=[pl.BlockSpec((tm, k), lambda i, j: (i, 0)),
                  pl.BlockSpec((k, tn), lambda i, j: (0, j))],
        out_specs=pl.BlockSpec((tm, tn), lambda i, j: (i, j)),
        out_shape=jax.ShapeDtypeStruct((m, n), BF16),
        compiler_params=_params(2),
        name="in_proj",
    )(x, w)


def _attn_body(sink_ref, q_ref, kp_ref, kc_ref, kn_ref, vp_ref, vc_ref, vn_ref, bias_ref, o_ref,
               *, n_kv, group, hd, geom):
    i = pl.program_id(0)
    n_blk_a, bps_a, bps_b = geom
    in_a = i < n_blk_a
    pos = jnp.where(in_a, i % bps_a, (i - n_blk_a) % bps_b)
    bps = jnp.where(in_a, bps_a, bps_b)
    first_pen = jnp.where(pos == 0, MASK_VALUE, 0.0).astype(F32)
    last_pen = jnp.where(pos == bps - 1, MASK_VALUE, 0.0).astype(F32)
    lane = lax.broadcasted_iota(jnp.int32, (1, 3 * ATTN_BLOCK), 1)
    pen = jnp.where(lane < ATTN_BLOCK, first_pen, jnp.where(lane >= 2 * ATTN_BLOCK, last_pen, 0.0))
    nq = ATTN_BLOCK
    for g in range(n_kv):
        cs = slice(g * hd, (g + 1) * hd)
        k3 = jnp.concatenate([kp_ref[:, cs], kc_ref[:, cs], kn_ref[:, cs]], axis=0)
        v3 = jnp.concatenate([vp_ref[:, cs], vc_ref[:, cs], vn_ref[:, cs]], axis=0)
        heads = range(g * group, (g + 1) * group)
        q4 = jnp.concatenate([q_ref[:, h * hd:(h + 1) * hd] for h in heads], axis=0)
        s4 = lax.dot_general(q4, k3, (((1,), (1,)), ((), ())), preferred_element_type=F32)
        ps, rdens = [], []
        for jj, h in enumerate(heads):
            s = s4[jj * nq:(jj + 1) * nq] + bias_ref[h] + pen
            sink = sink_ref[h] * LOG2E
            m = jnp.maximum(jnp.max(s, axis=-1, keepdims=True), sink)
            p = jnp.exp2(s - m)
            rdens.append(1.0 / (jnp.sum(p, axis=-1, keepdims=True) + jnp.exp2(sink - m)))
            ps.append(p.astype(BF16))
        o4 = jnp.dot(jnp.concatenate(ps, axis=0), v3, preferred_element_type=F32)
        for jj, h in enumerate(heads):
            o_ref[:, h * hd:(h + 1) * hd] = (o4[jj * nq:(jj + 1) * nq] * rdens[jj]).astype(o_ref.dtype)


def _attention(proj, sink, bias, n_heads, n_kv, hd, geom):
    t = proj.shape[0]
    nb = t // ATTN_BLOCK
    q_w, kv_w = n_heads * hd, n_kv * hd
    kblk, vblk = q_w // kv_w, q_w // kv_w + 1
    prev = lambda i: jnp.maximum(i - 1, 0)
    nxt = lambda i: jnp.minimum(i + 1, nb - 1)
    body = functools.partial(_attn_body, n_kv=n_kv, group=n_heads // n_kv, hd=hd, geom=geom)
    return pl.pallas_call(
        body,
        grid=(nb,),
        in_specs=[pl.BlockSpec(memory_space=pltpu.SMEM),
                  pl.BlockSpec((ATTN_BLOCK, q_w), lambda i: (i, 0)),
                  pl.BlockSpec((ATTN_BLOCK, kv_w), lambda i: (prev(i), kblk)),
                  pl.BlockSpec((ATTN_BLOCK, kv_w), lambda i: (i, kblk)),
                  pl.BlockSpec((ATTN_BLOCK, kv_w), lambda i: (nxt(i), kblk)),
                  pl.BlockSpec((ATTN_BLOCK, kv_w), lambda i: (prev(i), vblk)),
                  pl.BlockSpec((ATTN_BLOCK, kv_w), lambda i: (i, vblk)),
                  pl.BlockSpec((ATTN_BLOCK, kv_w), lambda i: (nxt(i), vblk)),
                  pl.BlockSpec((n_heads, ATTN_BLOCK, 3 * ATTN_BLOCK), lambda i: (0, 0, 0))],
        out_specs=pl.BlockSpec((ATTN_BLOCK, q_w), lambda i: (i, 0)),
        out_shape=jax.ShapeDtypeStruct((t, q_w), BF16),
        compiler_params=_params(1),
        name="attention",
    )(sink, proj, proj, proj, proj, proj, proj, proj, bias)


def _attn_bias(n_heads):
    slopes = jnp.exp2(-8.0 * jnp.arange(1, n_heads + 1, dtype=F32) / n_heads)
    rel = jnp.arange(ATTN_BLOCK)[:, None] - (jnp.arange(3 * ATTN_BLOCK)[None, :] - ATTN_BLOCK)
    dist = jnp.abs(rel).astype(F32)
    bias = -(slopes * LOG2E)[:, None, None] * dist[None]
    return jnp.where((jnp.abs(rel) <= ATTN_BLOCK)[None], bias, MASK_VALUE)


def _ssm_tables(a_re, a_im, log_dt, b_re, b_im, c_re, c_im, d_skip):
    L = SSM_CHUNK
    hp = lax.Precision.HIGHEST
    g_n, p_n, gc = a_re.shape[1], a_re.shape[2], b_re.shape[3]
    dt = jnp.exp(log_dt.astype(F32))[..., None]
    ar, ai = a_re.astype(F32), a_im.astype(F32)
    mag = jnp.exp(ar * dt)
    lr, li = mag * jnp.cos(ai * dt), mag * jnp.sin(ai * dt)
    den = ar * ar + ai * ai
    zr = ((lr - 1.0) * ar + li * ai) / den
    zi = (li * ar - (lr - 1.0) * ai) / den
    br, bi = b_re.astype(F32), b_im.astype(F32)
    bbr = zr[..., None] * br - zi[..., None] * bi
    bbi = zr[..., None] * bi + zi[..., None] * br
    tau = jnp.arange(L + 1, dtype=F32)[:, None, None, None]
    pmag = jnp.exp(tau * (ar * dt)[None])
    ang = tau * (ai * dt)[None]
    pr, pi = pmag * jnp.cos(ang), pmag * jnp.sin(ang)
    cr, ci = c_re.astype(F32), c_im.astype(F32)

    def lag_kernel(r):
        wr = pr[:L, r, :, :, None] * bbr[r][None] - pi[:L, r, :, :, None] * bbi[r][None]
        wi = pr[:L, r, :, :, None] * bbi[r][None] + pi[:L, r, :, :, None] * bbr[r][None]
        return (jnp.einsum('gop,tgpi->giot', cr, wr, precision=hp)
                - jnp.einsum('gop,tgpi->giot', ci, wi, precision=hp))

    kf, kb = lag_kernel(0), lag_kernel(1)
    eye = jnp.eye(gc, dtype=F32)
    center = kf[..., 0] + kb[..., 0] + eye[None] * d_skip.astype(F32).reshape(g_n, 1, gc)
    mvec = jnp.concatenate([jnp.zeros_like(center)[..., None], jnp.flip(kb[..., 1:], axis=-1),
                            center[..., None], kf[..., 1:]], axis=-1)
    mvec = mvec.reshape(g_n, gc * gc, 2 * L)

    pf_r, pf_i = jnp.flip(pr[:L, 0], axis=0), jnp.flip(pi[:L, 0], axis=0)
    pb_r, pb_i = pr[:L, 1], pi[:L, 1]

    def in_map(p_r, p_i, r):
        re = p_r[:, :, :, None] * bbr[r][None] - p_i[:, :, :, None] * bbi[r][None]
        im = p_r[:, :, :, None] * bbi[r][None] + p_i[:, :, :, None] * bbr[r][None]
        return jnp.transpose(re, (1, 3, 0, 2)), jnp.transpose(im, (1, 3, 0, 2))

    fr, fi = in_map(pf_r, pf_i, 0)
    rr, ri = in_map(pb_r, pb_i, 1)
    w_in = jnp.concatenate([fr, rr, fi, ri], axis=-1).reshape(g_n, gc * L, 4 * p_n)

    def out_map(p_r, p_i):
        a_r = cr[None] * p_r[:, :, None, :] - ci[None] * p_i[:, :, None, :]
        a_i = cr[None] * p_i[:, :, None, :] + ci[None] * p_r[:, :, None, :]
        return jnp.transpose(a_r, (1, 3, 2, 0)), jnp.transpose(-a_i, (1, 3, 2, 0))

    of_r, of_i = out_map(pr[1:L + 1, 0], pi[1:L + 1, 0])
    ob_r, ob_i = out_map(jnp.flip(pr[1:L + 1, 1], axis=0), jnp.flip(pi[1:L + 1, 1], axis=0))
    w_out = jnp.concatenate([of_r, ob_r, of_i, ob_i], axis=1).reshape(g_n, 4 * p_n, gc * L)
    lam = jnp.stack([jnp.concatenate([pr[L, 0], pr[L, 1]], axis=-1),
                     jnp.concatenate([pi[L, 0], pi[L, 1]], axis=-1)], axis=1)
    return mvec, w_in.astype(BF16), w_out.astype(BF16), lam


def _ssm_body(u_ref, mvec_ref, win_ref, wout_ref, lam_ref, o_ref, tz_ref, sr_ref, si_ref,
              xr_ref, xi_ref, *, gc, p2, segments):
    L = SSM_CHUNK
    nck = u_ref.shape[1]
    row = lax.broadcasted_iota(jnp.int32, (L, L), 0)
    col = lax.broadcasted_iota(jnp.int32, (L, L), 1)
    upper = col >= row

    def build(i, carry):
        for o in range(gc):
            vec = mvec_ref[0, pl.ds(i * gc + o, 1), :]
            lo = jnp.broadcast_to(vec[:, :L], (L, L))
            hi = jnp.broadcast_to(vec[:, L:], (L, L))
            r_lo = pltpu.roll(lo, 0, 1, stride=1, stride_axis=0)
            r_hi = pltpu.roll(hi, 0, 1, stride=1, stride_axis=0)
            blk = jnp.where(upper, r_hi, r_lo)
            tz_ref[pl.ds(pl.multiple_of(i * L, L), L), o * L:(o + 1) * L] = blk.astype(BF16)
        return carry

    lax.fori_loop(0, gc, build, 0)

    lhs = jnp.concatenate([u_ref[i] for i in range(gc)], axis=-1)
    s = jnp.dot(lhs, win_ref[0], preferred_element_type=F32)
    sr_ref[...] = s[:, 0:p2]
    si_ref[...] = s[:, p2:2 * p2]
    lam_r, lam_i = lam_ref[0, 0:1, :], lam_ref[0, 1:2, :]
    fwd_lane = lax.broadcasted_iota(jnp.int32, (1, p2), 1) < p2 // 2

    for base, nseq, nc in segments:
        cr = jnp.zeros((nseq, p2), F32)
        ci = jnp.zeros((nseq, p2), F32)
        for c in range(nc):
            rows = pl.ds(base + c, nseq, stride=nc) if nseq > 1 else pl.ds(base + c, 1)
            xr_ref[rows, :] = cr
            xi_ref[rows, :] = ci
            sr, si = sr_ref[rows, :], si_ref[rows, :]
            cr, ci = sr + lam_r * cr - lam_i * ci, si + lam_r * ci + lam_i * cr
        cr = jnp.zeros((nseq, p2), F32)
        ci = jnp.zeros((nseq, p2), F32)
        for c in range(nc - 1, -1, -1):
            rows = pl.ds(base + c, nseq, stride=nc) if nseq > 1 else pl.ds(base + c, 1)
            xr_ref[rows, :] = jnp.where(fwd_lane, xr_ref[rows, :], cr)
            xi_ref[rows, :] = jnp.where(fwd_lane, xi_ref[rows, :], ci)
            sr, si = sr_ref[rows, :], si_ref[rows, :]
            cr, ci = sr + lam_r * cr - lam_i * ci, si + lam_r * ci + lam_i * cr

    y = jnp.dot(lhs, tz_ref[...], preferred_element_type=F32)
    x_carry = jnp.concatenate([xr_ref[...], xi_ref[...]], axis=-1).astype(BF16)
    y = y + jnp.dot(x_carry, wout_ref[0], preferred_element_type=F32)
    for o in range(gc):
        o_ref[o] = jax.nn.gelu(y[:, o * L:(o + 1) * L]).astype(o_ref.dtype)


def _ssm(u3, tables, segments):
    mvec, w_in, w_out, lam = tables
    ds, nck, L = u3.shape
    g_n = mvec.shape[0]
    gc = ds // g_n
    p2 = lam.shape[-1]
    body = functools.partial(_ssm_body, gc=gc, p2=p2, segments=segments)
    return pl.pallas_call(
        body,
        grid=(g_n,),
        in_specs=[pl.BlockSpec((gc, nck, L), lambda g: (g, 0, 0)),
                  pl.BlockSpec((1, gc * gc, 2 * L), lambda g: (g, 0, 0)),
                  pl.BlockSpec((1, gc * L, 2 * p2), lambda g: (g, 0, 0)),
                  pl.BlockSpec((1, 2 * p2, gc * L), lambda g: (g, 0, 0)),
                  pl.BlockSpec((1, 2, p2), lambda g: (g, 0, 0))],
        out_specs=pl.BlockSpec((gc, nck, L), lambda g: (g, 0, 0)),
        out_shape=jax.ShapeDtypeStruct((ds, nck, L), BF16),
        scratch_shapes=[pltpu.VMEM((gc * L, gc * L), BF16)] + [pltpu.VMEM((nck, p2), F32)] * 4,
        compiler_params=_params(1),
        name="ssm",
    )(u3, mvec, w_in, w_out, lam)


def _glu_body(x_ref, wa_ref, wb_ref, o_ref):
    x = x_ref[...]
    a = jnp.dot(x, wa_ref[...], preferred_element_type=F32)
    b = jnp.dot(x, wb_ref[...], preferred_element_type=F32)
    o_ref[...] = (a * jax.nn.sigmoid(b)).astype(o_ref.dtype)


def _glu(x, w, tm=1024, tn=512):
    m, k = x.shape
    n = w.shape[1] // 2
    tm, tn = min(tm, m), min(tn, n)
    assert m % tm == 0 and n % tn == 0
    nj = n // tn
    return pl.pallas_call(
        _glu_body,
        grid=(m // tm, nj),
        in_specs=[pl.BlockSpec((tm, k), lambda i, j: (i, 0)),
                  pl.BlockSpec((k, tn), lambda i, j: (0, j)),
                  pl.BlockSpec((k, tn), lambda i, j: (0, j + nj))],
        out_specs=pl.BlockSpec((tm, tn), lambda i, j: (i, j)),
        out_shape=jax.ShapeDtypeStruct((m, n), BF16),
        compiler_params=_params(2),
        name="glu",
    )(x, w, w)


def _mix_body(a_ref, w_ref, ga_ref, gb_ref, yb_ref, o_ref):
    ya = jnp.dot(a_ref[...], w_ref[...], preferred_element_type=F32)
    o_ref[...] = (ga_ref[...].astype(F32) * ya
                  + gb_ref[...].astype(F32) * yb_ref[...].astype(F32)).astype(o_ref.dtype)


def _mix(attn, w_ao, proj, ga_col0, y_b, tm=1024, tn=512):
    m, k = attn.shape
    n = w_ao.shape[1]
    tm, tn = min(tm, m), min(tn, n)
    assert m % tm == 0 and n % tn == 0 and ga_col0 % tn == 0
    ga0, gb0 = ga_col0 // tn, (ga_col0 + n) // tn
    return pl.pallas_call(
        _mix_body,
        grid=(m // tm, n // tn),
        in_specs=[pl.BlockSpec((tm, k), lambda i, j: (i, 0)),
                  pl.BlockSpec((k, tn), lambda i, j: (0, j)),
                  pl.BlockSpec((tm, tn), lambda i, j: (i, ga0 + j)),
                  pl.BlockSpec((tm, tn), lambda i, j: (i, gb0 + j)),
                  pl.BlockSpec((tm, tn), lambda i, j: (i, j))],
        out_specs=pl.BlockSpec((tm, tn), lambda i, j: (i, j)),
        out_shape=jax.ShapeDtypeStruct((m, n), BF16),
        compiler_params=_params(2),
        name="mix",
    )(attn, w_ao, proj, proj, y_b)


def _ln1_router_body(mixed_ref, wmix_ref, x_ref, g_ref, b_ref, wr_ref, br_ref, tri_ref,
                     x1_ref, info_ref, cnt_ref, run_ref, *, alpha, sub):
    i = pl.program_id(0)

    @pl.when(i == 0)
    def _():
        run_ref[...] = jnp.zeros_like(run_ref)

    run = run_ref[...]
    lane = lax.broadcasted_iota(jnp.int32, (sub, LANES), 1).astype(F32)
    for r in range(x_ref.shape[0] // sub):
        rows = slice(r * sub, (r + 1) * sub)
        z = alpha * x_ref[rows, :].astype(F32) + jnp.dot(mixed_ref[rows, :], wmix_ref[...],
                                                          preferred_element_type=F32)
        y = _layer_norm_rows(z, g_ref[...], b_ref[...])
        x1_ref[rows, :] = y
        y_hi = y.astype(BF16)
        y_lo = (y - y_hi.astype(F32)).astype(BF16)
        d_hi = jnp.dot(y_hi, wr_ref[...], preferred_element_type=F32)
        d_lo = jnp.dot(y_lo, wr_ref[:, 0:LANES], preferred_element_type=F32)
        logits = d_hi[:, 0:LANES] + d_hi[:, LANES:2 * LANES] + d_lo + br_ref[...]
        rem = logits
        idxs, vals = [], []
        for _ in range(TOP_K):
            mx = jnp.max(rem, axis=-1, keepdims=True)
            idx = jnp.min(jnp.where(rem == mx, lane, float(LANES)), axis=-1, keepdims=True)
            idxs.append(idx)
            vals.append(mx)
            rem = jnp.where(lane == idx, -jnp.inf, rem)
        exps = [jnp.exp(v - vals[0]) for v in vals]
        rden = 1.0 / (exps[0] + exps[1] + exps[2] + exps[3])
        onehot = jnp.zeros((sub, LANES), F32)
        for idx in idxs:
            onehot = onehot + jnp.where(lane == idx, 1.0, 0.0)
        prefix = jnp.dot(tri_ref[...], onehot.astype(BF16), preferred_element_type=F32) + run
        info = jnp.zeros((sub, LANES), F32)
        for k in range(TOP_K):
            rank = jnp.sum(jnp.where(lane == idxs[k], prefix, 0.0), axis=-1, keepdims=True)
            info = info + jnp.where(lane == float(k), idxs[k], 0.0)
            info = info + jnp.where(lane == float(TOP_K + k), exps[k] * rden, 0.0)
            info = info + jnp.where(lane == float(2 * TOP_K + k), rank, 0.0)
        info_ref[rows, :] = info
        run = run + jnp.sum(onehot, axis=0, keepdims=True)
    run_ref[...] = run
    cnt_ref[...] = run


def _ln1_router(mixed, w_mix, x, ln_g, ln_b, w_r, b_r, alpha, tm=512, sub=256):
    t, d = mixed.shape
    e = w_r.shape[1]
    tm = min(tm, t)
    sub = min(sub, tm)
    assert t % tm == 0 and tm % sub == 0 and e <= LANES
    wr_pad = jnp.zeros((d, LANES), F32).at[:, :e].set(w_r.astype(F32))
    wr_hi = wr_pad.astype(BF16)
    wr_lo = (wr_pad - wr_hi.astype(F32)).astype(BF16)
    wr_split = jnp.concatenate([wr_hi, wr_lo], axis=1)
    br_pad = jnp.full((1, LANES), MASK_VALUE, F32).at[0, :e].set(b_r.astype(F32))
    tri = jnp.tril(jnp.ones((sub, sub), BF16), -1)
    row = lambda v: v.astype(F32).reshape(1, d)
    const = lambda i: (0, 0)
    return pl.pallas_call(
        functools.partial(_ln1_router_body, alpha=alpha, sub=sub),
        grid=(t // tm,),
        in_specs=[pl.BlockSpec((tm, d), lambda i: (i, 0)),
                  pl.BlockSpec((d, d), const),
                  pl.BlockSpec((tm, d), lambda i: (i, 0)),
                  pl.BlockSpec((1, d), const), pl.BlockSpec((1, d), const),
                  pl.BlockSpec((d, 2 * LANES), const), pl.BlockSpec((1, LANES), const),
                  pl.BlockSpec((sub, sub), const)],
        out_specs=[pl.BlockSpec((tm, d), lambda i: (i, 0)),
                   pl.BlockSpec((tm, LANES), lambda i: (i, 0)),
                   pl.BlockSpec((1, LANES), const)],
        out_shape=[jax.ShapeDtypeStruct((t, d), F32),
                   jax.ShapeDtypeStruct((t, LANES), F32),
                   jax.ShapeDtypeStruct((1, LANES), F32)],
        scratch_shapes=[pltpu.VMEM((1, LANES), F32)],
        compiler_params=_params(1),
        name="ln1_router",
    )(mixed, w_mix, x, row(ln_g), row(ln_b), wr_split, br_pad, tri)


def _row_copy(src, src_row, dst, dst_row, sem):
    return pltpu.make_async_copy(src.at[pl.ds(src_row, 1), :], dst.at[pl.ds(dst_row, 1), :], sem)


def _dispatch_body(pos_ref, x_ref, xs_in_hbm, xs_hbm, pos_smem, sem_idx, sem_rows, *, tm):
    del xs_in_hbm
    idx_cp = pltpu.make_async_copy(pos_ref.at[0, 0], pos_smem, sem_idx)
    idx_cp.start()
    idx_cp.wait()

    def issue(r, carry):
        for k in range(TOP_K):
            _row_copy(x_ref, r, xs_hbm, pos_smem[k * tm + r], sem_rows).start()
        return carry

    lax.fori_loop(0, tm, issue, 0)
    for k in range(TOP_K):
        pltpu.make_async_copy(x_ref, xs_hbm.at[pl.ds(0, tm), :], sem_rows).wait()


def _dispatch(x1, pos_tiles, xs_buf, tm):
    t, d = x1.shape
    return pl.pallas_call(
        functools.partial(_dispatch_body, tm=tm),
        grid=(t // tm,),
        in_specs=[pl.BlockSpec((1, 1, TOP_K * tm), lambda i: (i, 0, 0)),
                  pl.BlockSpec((tm, d), lambda i: (i, 0)),
                  pl.BlockSpec(memory_space=pl.ANY)],
        out_specs=pl.BlockSpec(memory_space=pl.ANY),
        out_shape=jax.ShapeDtypeStruct(xs_buf.shape, F32),
        input_output_aliases={2: 0},
        scratch_shapes=[pltpu.SMEM((TOP_K * tm,), jnp.int32),
                        pltpu.SemaphoreType.DMA(()), pltpu.SemaphoreType.DMA(())],
        compiler_params=_params(1),
        name="moe_dispatch",
    )(pos_tiles, x1, xs_buf)


def _expert_body(be_ref, nv_ref, xs_ref, w1g_ref, w1u_ref, b1g_ref, b1u_ref, w2_ref, b2_ref, o_ref):
    b = pl.program_id(0)
    f = pl.program_id(1)
    nv = nv_ref[b]

    @pl.when(jnp.logical_and(nv == 0, f == 0))
    def _():
        o_ref[...] = jnp.zeros_like(o_ref)

    @pl.when(nv > 0)
    def _():
        tm = xs_ref.shape[0]
        row = lax.broadcasted_iota(jnp.int32, (tm, 1), 0)
        x = jnp.where(row < nv, xs_ref[...], 0.0).astype(BF16)
        hg = jnp.dot(x, w1g_ref[0], preferred_element_type=F32) + b1g_ref[0]
        hu = jnp.dot(x, w1u_ref[0], preferred_element_type=F32) + b1u_ref[0]
        gate = jnp.minimum(hg, SWIGLU_LIMIT)
        up = jnp.clip(hu, -SWIGLU_LIMIT, SWIGLU_LIMIT)
        act = (up + 1.0) * (gate * jax.nn.sigmoid(SWIGLU_ALPHA * gate))
        y = jnp.dot(act.astype(BF16), w2_ref[0], preferred_element_type=F32)

        @pl.when(f == 0)
        def _():
            o_ref[...] = y + b2_ref[0]

        @pl.when(f > 0)
        def _():
            o_ref[...] += y


def _experts(xs, block_e, block_nv, w1, b1, w2, b2, tm, tf=1024):
    n_slots, d = xs.shape
    e, ff = w2.shape[0], w2.shape[1]
    tf = min(tf, ff)
    nf = ff // tf
    assert ff % tf == 0 and n_slots % tm == 0
    b1r = b1.astype(F32).reshape(e, 1, 2 * ff)
    b2r = b2.astype(F32).reshape(e, 1, d)
    fi = lambda b, f, nv: jnp.where(nv[b] > 0, f, nf - 1)
    grid_spec = pltpu.PrefetchScalarGridSpec(
        num_scalar_prefetch=2,
        grid=(n_slots // tm, nf),
        in_specs=[pl.BlockSpec((tm, d), lambda b, f, be, nv: (b, 0)),
                  pl.BlockSpec((1, d, tf), lambda b, f, be, nv: (be[b], 0, fi(b, f, nv))),
                  pl.BlockSpec((1, d, tf), lambda b, f, be, nv: (be[b], 0, nf + fi(b, f, nv))),
                  pl.BlockSpec((1, 1, tf), lambda b, f, be, nv: (be[b], 0, fi(b, f, nv))),
                  pl.BlockSpec((1, 1, tf), lambda b, f, be, nv: (be[b], 0, nf + fi(b, f, nv))),
                  pl.BlockSpec((1, tf, d), lambda b, f, be, nv: (be[b], fi(b, f, nv), 0)),
                  pl.BlockSpec((1, 1, d), lambda b, f, be, nv: (be[b], 0, 0))],
        out_specs=pl.BlockSpec((tm, d), lambda b, f, be, nv: (b, 0)),
    )
    return pl.pallas_call(
        _expert_body,
        grid_spec=grid_spec,
        out_shape=jax.ShapeDtypeStruct((n_slots, d), F32),
        compiler_params=_params(2),
        name="moe_experts",
    )(block_e, block_nv, xs, w1, w1, b1r, b1r, w2, b2r)


def _combine_body(pos_ref, pos_next_ref, info_ref, x1_ref, g_ref, b_ref, ys_hbm, x2_ref, x2b_ref,
                  pos_smem, buf, sem_idx, sem_rows, *, tm, alpha):
    i = pl.program_id(0)
    n = pl.num_programs(0)
    slot = i % 2

    def start_gather(pos_vmem, dst_slot):
        idx_cp = pltpu.make_async_copy(pos_vmem.at[0, 0], pos_smem, sem_idx)
        idx_cp.start()
        idx_cp.wait()

        def issue(r, carry):
            for k in range(TOP_K):
                _row_copy(ys_hbm, pos_smem[k * tm + r], buf.at[dst_slot], k * tm + r,
                          sem_rows.at[dst_slot]).start()
            return carry

        lax.fori_loop(0, tm, issue, 0)

    @pl.when(i == 0)
    def _():
        start_gather(pos_ref, 0)

    @pl.when(i + 1 < n)
    def _():
        start_gather(pos_next_ref, 1 - slot)

    pltpu.make_async_copy(ys_hbm.at[pl.ds(0, TOP_K * tm), :], buf.at[slot], sem_rows.at[slot]).wait()

    info = info_ref[...]
    f = jnp.zeros(x1_ref.shape, F32)
    for k in range(TOP_K):
        f = f + info[:, TOP_K + k:TOP_K + k + 1] * buf[slot, k * tm:(k + 1) * tm, :]
    y = _layer_norm_rows(alpha * x1_ref[...] + f, g_ref[...], b_ref[...])
    x2_ref[...] = y
    x2b_ref[...] = y.astype(BF16)


def _combine_ln2(ys, pos_tiles, info, x1, ln_g, ln_b, alpha, tm):
    t, d = x1.shape
    nt = t // tm
    row = lambda v: v.astype(F32).reshape(1, d)
    const = lambda i: (0, 0)
    return pl.pallas_call(
        functools.partial(_combine_body, tm=tm, alpha=alpha),
        grid=(nt,),
        in_specs=[pl.BlockSpec((1, 1, TOP_K * tm), lambda i: (i, 0, 0)),
                  pl.BlockSpec((1, 1, TOP_K * tm), lambda i: (jnp.minimum(i + 1, nt - 1), 0, 0)),
                  pl.BlockSpec((tm, LANES), lambda i: (i, 0)),
                  pl.BlockSpec((tm, d), lambda i: (i, 0)),
                  pl.BlockSpec((1, d), const), pl.BlockSpec((1, d), const),
                  pl.BlockSpec(memory_space=pl.ANY)],
        out_specs=[pl.BlockSpec((tm, d), lambda i: (i, 0)),
                   pl.BlockSpec((tm, d), lambda i: (i, 0))],
        out_shape=[jax.ShapeDtypeStruct((t, d), F32), jax.ShapeDtypeStruct((t, d), BF16)],
        scratch_shapes=[pltpu.SMEM((TOP_K * tm,), jnp.int32),
                        pltpu.VMEM((2, TOP_K * tm, d), F32),
                        pltpu.SemaphoreType.DMA(()), pltpu.SemaphoreType.DMA((2,))],
        compiler_params=_params(1),
        name="moe_combine_ln2",
    )(pos_tiles, pos_tiles, info, x1, row(ln_g), row(ln_b), ys)


def _pos_tiles(pos, tm):
    t = pos.shape[0]
    return pos.reshape(t // tm, tm, TOP_K).transpose(0, 2, 1).reshape(t // tm, 1, TOP_K * tm)


def _route(info, counts, n_exp, block):
    t = info.shape[0]
    idx = info[:, 0:TOP_K].astype(jnp.int32)
    rank = info[:, 2 * TOP_K:3 * TOP_K].astype(jnp.int32)
    cnt = counts[0, :n_exp].astype(jnp.int32)
    padded = (cnt + block - 1) // block * block
    pends = jnp.cumsum(padded)
    pstarts = pends - padded
    pos = jnp.take(pstarts, idx, axis=0) + rank
    n_blocks = _num_slot_blocks(t, n_exp, block)
    bstart = jnp.arange(n_blocks, dtype=jnp.int32) * block
    block_e = jnp.clip(jnp.searchsorted(pends, bstart, side='right'), 0, n_exp - 1).astype(jnp.int32)
    block_nv = jnp.clip(jnp.take(pstarts + cnt, block_e) - bstart, 0, block).astype(jnp.int32)
    return pos, block_e, block_nv


def _num_slot_blocks(t, n_exp, block):
    return (t * TOP_K) // block + n_exp


def _encoder_layer(x, xb, xs_buf, geom, w_in, sink, w_ao, ssm_tabs, w_glu, w_mix, ln1_g, ln1_b,
                   w_r, b_r, w1, b1, w2, b2, ln2_g, ln2_b, *, layer, alpha, dims, slot_block,
                   tok_tile):
    n_heads, n_kv, hd, d_ssm, d = dims
    t = x.shape[0]
    q_w, kv_w = n_heads * hd, n_kv * hd
    u_col0 = q_w + 2 * kv_w
    gate_col0 = u_col0 + d_ssm
    attn_geom, ssm_segments = geom

    proj = _in_proj(xb, w_in, gate_col0)
    attn = _attention(proj, sink, _attn_bias(n_heads), n_heads, n_kv, hd, attn_geom)
    u3 = proj[:, u_col0:gate_col0].T.reshape(d_ssm, t // SSM_CHUNK, SSM_CHUNK)
    gy = _ssm(u3, ssm_tabs, ssm_segments).reshape(d_ssm, t).T
    y_b = _glu(gy, w_glu)
    mixed = _mix(attn, w_ao, proj, gate_col0, y_b)
    x1, info, counts = _ln1_router(mixed, w_mix, x, ln1_g, ln1_b, w_r, b_r, alpha)

    n_exp = w_r.shape[1]
    pos, block_e, block_nv = _route(info, counts, n_exp, slot_block)
    pos_t = _pos_tiles(pos, tok_tile)
    xs = _dispatch(x1, pos_t, xs_buf, tok_tile)
    ys = _experts(xs, block_e + layer * n_exp, block_nv, w1, b1, w2, b2, slot_block)
    x2, x2b = _combine_ln2(ys, pos_t, info, x1, ln2_g, ln2_b, alpha, tok_tile)
    return x2, x2b, xs


def kernel(x_prompt, x_sample, w_in, attn_sink, w_attn_out, ssm_a_re, ssm_a_im, ssm_log_dt, ssm_b_re, ssm_b_im, ssm_c_re, ssm_c_im, ssm_d, w_glu, w_mix_out, ln1_g, ln1_b, router_w, router_b, expert_w1, expert_b1, expert_w2, expert_b2, ln2_g, ln2_b):
    depth = w_in.shape[0]
    bp, sp, d = x_prompt.shape
    bs, ss, _ = x_sample.shape
    n_heads = attn_sink.shape[1]
    hd = d // n_heads
    d_ssm = ssm_a_re.shape[2] * ssm_b_re.shape[4]
    kv_w = (w_in.shape[2] - d - d_ssm - 2 * d) // 2
    dims = (n_heads, kv_w // hd, hd, d_ssm, d)
    alpha = (2.0 * depth) ** 0.25
    assert sp % ATTN_BLOCK == 0 and ss % ATTN_BLOCK == 0 and ATTN_BLOCK == SSM_CHUNK

    tp = bp * sp
    attn_geom = (tp // ATTN_BLOCK, sp // ATTN_BLOCK, ss // ATTN_BLOCK)
    ssm_segments = ((0, bp, sp // SSM_CHUNK), (tp // SSM_CHUNK, bs, ss // SSM_CHUNK))
    geom = (attn_geom, ssm_segments)

    x = jnp.concatenate([x_prompt.reshape(tp, d), x_sample.reshape(bs * ss, d)], axis=0)
    xb = x
    t = x.shape[0]
    slot_block, tok_tile = min(SLOT_BLOCK, t), min(TOKEN_TILE, t)
    n_slots = _num_slot_blocks(t, router_w.shape[2], slot_block) * slot_block
    xs_buf = jnp.zeros((n_slots, d), F32)
    col_scale = jnp.where(jnp.arange(w_in.shape[2]) < n_heads * hd, hd ** -0.5 * LOG2E, 1.0)
    w_in_b = (w_in * col_scale.astype(F32)).astype(BF16)
    n_exp, ff = expert_w2.shape[1], expert_w2.shape[2]
    w1_all = expert_w1.astype(BF16).reshape(depth * n_exp, d, 2 * ff)
    w2_all = expert_w2.astype(BF16).reshape(depth * n_exp, ff, d)
    b1_all = expert_b1.reshape(depth * n_exp, 2 * ff)
    b2_all = expert_b2.reshape(depth * n_exp, d)
    for l in range(depth):
        tabs = _ssm_tables(ssm_a_re[l], ssm_a_im[l], ssm_log_dt[l], ssm_b_re[l], ssm_b_im[l],
                           ssm_c_re[l], ssm_c_im[l], ssm_d[l])
        x, xb, xs_buf = _encoder_layer(
            x, xb, xs_buf, geom, w_in_b[l], attn_sink[l].astype(F32),
            w_attn_out[l].astype(BF16), tabs, w_glu[l].astype(BF16), w_mix_out[l].astype(BF16),
            ln1_g[l], ln1_b[l], router_w[l], router_b[l], w1_all, b1_all, w2_all, b2_all,
            ln2_g[l], ln2_b[l], layer=l, alpha=alpha, dims=dims,
            slot_block=slot_block, tok_tile=tok_tile)
    return (x[:tp].reshape(bp, sp, d), x[tp:].reshape(bs, ss, d))
```

```python
import functools
import math

import jax
import jax.numpy as jnp
from jax import lax
from jax.experimental import pallas as pl
from jax.experimental.pallas import tpu as pltpu

F32 = jnp.float32
BF16 = jnp.bfloat16

LANES = 128
SUBLANES = 8
ATTN_BLOCK = 128
SSM_CHUNK = 128
TOP_K = 4
LN_EPS = 1e-5
SWIGLU_ALPHA = 1.702
SWIGLU_LIMIT = 7.0
MASK_VALUE = -1e30
LOG2E = math.log2(math.e)
SLOT_BLOCK = 512
TOKEN_TILE = 128
ROW_SUBTILES = 2
COL_SUBTILES = 2
VMEM_LIMIT_BYTES = 56 * 1024 * 1024


def _params(n_grid_dims):
    return pltpu.CompilerParams(dimension_semantics=("arbitrary",) * n_grid_dims,
                                vmem_limit_bytes=VMEM_LIMIT_BYTES)


def _layer_norm_rows(z, g, b):
    mu = jnp.mean(z, axis=-1, keepdims=True)
    zc = z - mu
    var = jnp.mean(zc * zc, axis=-1, keepdims=True)
    return zc * lax.rsqrt(var + LN_EPS) * g + b


def _proj_body(x_ref, w_ref, o_ref, *, gate, n_sub):
    x = x_ref[...].astype(BF16)
    sw = w_ref.shape[1] // n_sub
    for s in range(n_sub):
        cols = slice(s * sw, (s + 1) * sw)
        acc = jnp.dot(x, w_ref[:, cols], preferred_element_type=F32)
        if gate:
            acc = jax.nn.sigmoid(acc)
        o_ref[:, cols] = acc.astype(o_ref.dtype)


def _in_proj(x, w, col0, n, gate, tm=1024, tn=1024, n_sub=COL_SUBTILES):
    m, k = x.shape
    tm, tn = min(tm, m), min(tn, n)
    assert m % tm == 0 and n % tn == 0 and col0 % tn == 0 and tn % (n_sub * LANES) == 0
    j0 = col0 // tn
    return pl.pallas_call(
        functools.partial(_proj_body, gate=gate, n_sub=n_sub),
        grid=(m // tm, n // tn),
        in_specs=[pl.BlockSpec((tm, k), lambda i, j: (i, 0)),
                  pl.BlockSpec((k, tn), lambda i, j: (0, j0 + j))],
        out_specs=pl.BlockSpec((tm, tn), lambda i, j: (i, j)),
        out_shape=jax.ShapeDtypeStruct((m, n), BF16),
        compiler_params=_params(2),
        name="in_proj_gates" if gate else "in_proj",
    )(x, w)


def _attn_body(sink_ref, q_ref, kp_ref, kc_ref, kn_ref, vp_ref, vc_ref, vn_ref, bias_ref, o_ref,
               *, n_kv, group, hd, geom):
    i = pl.program_id(0)
    n_blk_a, bps_a, bps_b = geom
    in_a = i < n_blk_a
    pos = jnp.where(in_a, i % bps_a, (i - n_blk_a) % bps_b)
    bps = jnp.where(in_a, bps_a, bps_b)
    first_pen = jnp.where(pos == 0, MASK_VALUE, 0.0).astype(F32)
    last_pen = jnp.where(pos == bps - 1, MASK_VALUE, 0.0).astype(F32)
    lane = lax.broadcasted_iota(jnp.int32, (1, 3 * ATTN_BLOCK), 1)
    pen = jnp.where(lane < ATTN_BLOCK, first_pen, jnp.where(lane >= 2 * ATTN_BLOCK, last_pen, 0.0))
    nq = ATTN_BLOCK
    for g in range(n_kv):
        cs = slice(g * hd, (g + 1) * hd)
        k3 = jnp.concatenate([kp_ref[:, cs], kc_ref[:, cs], kn_ref[:, cs]], axis=0)
        v3 = jnp.concatenate([vp_ref[:, cs], vc_ref[:, cs], vn_ref[:, cs]], axis=0)
        heads = range(g * group, (g + 1) * group)
        q4 = jnp.concatenate([q_ref[:, h * hd:(h + 1) * hd] for h in heads], axis=0)
        s4 = lax.dot_general(q4, k3, (((1,), (1,)), ((), ())), preferred_element_type=F32)
        ps, rdens = [], []
        for jj, h in enumerate(heads):
            s = s4[jj * nq:(jj + 1) * nq] + bias_ref[h] + pen
            sink = sink_ref[h] * LOG2E
            m = jnp.maximum(jnp.max(s, axis=-1, keepdims=True), sink)
            p = jnp.exp2(s - m)
            rdens.append(1.0 / (jnp.sum(p, axis=-1, keepdims=True) + jnp.exp2(sink - m)))
            ps.append(p.astype(BF16))
        o4 = jnp.dot(jnp.concatenate(ps, axis=0), v3, preferred_element_type=F32)
        for jj, h in enumerate(heads):
            o_ref[:, h * hd:(h + 1) * hd] = (o4[jj * nq:(jj + 1) * nq] * rdens[jj]).astype(o_ref.dtype)


def _attention(proj, sink, bias, n_heads, n_kv, hd, geom):
    t = proj.shape[0]
    nb = t // ATTN_BLOCK
    q_w, kv_w = n_heads * hd, n_kv * hd
    kblk, vblk = q_w // kv_w, q_w // kv_w + 1
    prev = lambda i: jnp.maximum(i - 1, 0)
    nxt = lambda i: jnp.minimum(i + 1, nb - 1)
    body = functools.partial(_attn_body, n_kv=n_kv, group=n_heads // n_kv, hd=hd, geom=geom)
    return pl.pallas_call(
        body,
        grid=(nb,),
        in_specs=[pl.BlockSpec(memory_space=pltpu.SMEM),
                  pl.BlockSpec((ATTN_BLOCK, q_w), lambda i: (i, 0)),
                  pl.BlockSpec((ATTN_BLOCK, kv_w), lambda i: (prev(i), kblk)),
                  pl.BlockSpec((ATTN_BLOCK, kv_w), lambda i: (i, kblk)),
                  pl.BlockSpec((ATTN_BLOCK, kv_w), lambda i: (nxt(i), kblk)),
                  pl.BlockSpec((ATTN_BLOCK, kv_w), lambda i: (prev(i), vblk)),
                  pl.BlockSpec((ATTN_BLOCK, kv_w), lambda i: (i, vblk)),
                  pl.BlockSpec((ATTN_BLOCK, kv_w), lambda i: (nxt(i), vblk)),
                  pl.BlockSpec((n_heads, ATTN_BLOCK, 3 * ATTN_BLOCK), lambda i: (0, 0, 0))],
        out_specs=pl.BlockSpec((ATTN_BLOCK, q_w), lambda i: (i, 0)),
        out_shape=jax.ShapeDtypeStruct((t, q_w), BF16),
        compiler_params=_params(1),
        name="attention",
    )(sink, proj, proj, proj, proj, proj, proj, proj, bias)


def _attn_bias(n_heads):
    slopes = jnp.exp2(-8.0 * jnp.arange(1, n_heads + 1, dtype=F32) / n_heads)
    rel = jnp.arange(ATTN_BLOCK)[:, None] - (jnp.arange(3 * ATTN_BLOCK)[None, :] - ATTN_BLOCK)
    dist = jnp.abs(rel).astype(F32)
    bias = -(slopes * LOG2E)[:, None, None] * dist[None]
    return jnp.where((jnp.abs(rel) <= ATTN_BLOCK)[None], bias, MASK_VALUE)


def _ssm_tables(a_re, a_im, log_dt, b_re, b_im, c_re, c_im, d_skip):
    L = SSM_CHUNK
    hp = lax.Precision.HIGHEST
    g_n, p_n, gc = a_re.shape[1], a_re.shape[2], b_re.shape[3]
    dt = jnp.exp(log_dt.astype(F32))[..., None]
    ar, ai = a_re.astype(F32), a_im.astype(F32)
    mag = jnp.exp(ar * dt)
    lr, li = mag * jnp.cos(ai * dt), mag * jnp.sin(ai * dt)
    den = ar * ar + ai * ai
    zr = ((lr - 1.0) * ar + li * ai) / den
    zi = (li * ar - (lr - 1.0) * ai) / den
    br, bi = b_re.astype(F32), b_im.astype(F32)
    bbr = zr[..., None] * br - zi[..., None] * bi
    bbi = zr[..., None] * bi + zi[..., None] * br
    tau = jnp.arange(L + 1, dtype=F32)[:, None, None, None]
    pmag = jnp.exp(tau * (ar * dt)[None])
    ang = tau * (ai * dt)[None]
    pr, pi = pmag * jnp.cos(ang), pmag * jnp.sin(ang)
    cr, ci = c_re.astype(F32), c_im.astype(F32)

    def lag_kernel(r):
        wr = pr[:L, r, :, :, None] * bbr[r][None] - pi[:L, r, :, :, None] * bbi[r][None]
        wi = pr[:L, r, :, :, None] * bbi[r][None] + pi[:L, r, :, :, None] * bbr[r][None]
        return (jnp.einsum('gop,tgpi->giot', cr, wr, precision=hp)
                - jnp.einsum('gop,tgpi->giot', ci, wi, precision=hp))

    kf, kb = lag_kernel(0), lag_kernel(1)
    eye = jnp.eye(gc, dtype=F32)
    center = kf[..., 0] + kb[..., 0] + eye[None] * d_skip.astype(F32).reshape(g_n, 1, gc)
    mvec = jnp.concatenate([jnp.zeros_like(center)[..., None], jnp.flip(kb[..., 1:], axis=-1),
                            center[..., None], kf[..., 1:]], axis=-1)
    mvec = mvec.reshape(g_n, gc * gc, 2 * L)

    pf_r, pf_i = jnp.flip(pr[:L, 0], axis=0), jnp.flip(pi[:L, 0], axis=0)
    pb_r, pb_i = pr[:L, 1], pi[:L, 1]

    def in_map(p_r, p_i, r):
        re = p_r[:, :, :, None] * bbr[r][None] - p_i[:, :, :, None] * bbi[r][None]
        im = p_r[:, :, :, None] * bbi[r][None] + p_i[:, :, :, None] * bbr[r][None]
        return jnp.transpose(re, (1, 3, 0, 2)), jnp.transpose(im, (1, 3, 0, 2))

    fr, fi = in_map(pf_r, pf_i, 0)
    rr, ri = in_map(pb_r, pb_i, 1)
    w_in = jnp.concatenate([fr, rr, fi, ri], axis=-1).reshape(g_n, gc * L, 4 * p_n)

    def out_map(p_r, p_i):
        a_r = cr[None] * p_r[:, :, None, :] - ci[None] * p_i[:, :, None, :]
        a_i = cr[None] * p_i[:, :, None, :] + ci[None] * p_r[:, :, None, :]
        return jnp.transpose(a_r, (1, 3, 2, 0)), jnp.transpose(-a_i, (1, 3, 2, 0))

    of_r, of_i = out_map(pr[1:L + 1, 0], pi[1:L + 1, 0])
    ob_r, ob_i = out_map(jnp.flip(pr[1:L + 1, 1], axis=0), jnp.flip(pi[1:L + 1, 1], axis=0))
    w_out = jnp.concatenate([of_r, ob_r, of_i, ob_i], axis=1).reshape(g_n, 4 * p_n, gc * L)
    lam = jnp.stack([jnp.concatenate([pr[L, 0], pr[L, 1]], axis=-1),
                     jnp.concatenate([pi[L, 0], pi[L, 1]], axis=-1)], axis=1)
    return mvec, w_in.astype(BF16), w_out.astype(BF16), lam


def _ssm_body(u_ref, mvec_ref, win_ref, wout_ref, lam_ref, o_ref, tz_ref, sr_ref, si_ref,
              xr_ref, xi_ref, *, gc, p2, segments):
    L = SSM_CHUNK
    nck = u_ref.shape[1]
    row = lax.broadcasted_iota(jnp.int32, (L, L), 0)
    col = lax.broadcasted_iota(jnp.int32, (L, L), 1)
    upper = col >= row

    def build(i, carry):
        for o in range(gc):
            vec = mvec_ref[0, pl.ds(i * gc + o, 1), :]
            lo = jnp.broadcast_to(vec[:, :L], (L, L))
            hi = jnp.broadcast_to(vec[:, L:], (L, L))
            r_lo = pltpu.roll(lo, 0, 1, stride=1, stride_axis=0)
            r_hi = pltpu.roll(hi, 0, 1, stride=1, stride_axis=0)
            blk = jnp.where(upper, r_hi, r_lo)
            tz_ref[pl.ds(pl.multiple_of(i * L, L), L), o * L:(o + 1) * L] = blk.astype(BF16)
        return carry

    lax.fori_loop(0, gc, build, 0)

    lhs = jnp.concatenate([u_ref[i] for i in range(gc)], axis=-1)
    s = jnp.dot(lhs, win_ref[0], preferred_element_type=F32)
    sr_ref[...] = s[:, 0:p2]
    si_ref[...] = s[:, p2:2 * p2]
    lam_r, lam_i = lam_ref[0, 0:1, :], lam_ref[0, 1:2, :]
    fwd_lane = lax.broadcasted_iota(jnp.int32, (1, p2), 1) < p2 // 2

    for base, nseq, nc in segments:
        cr = jnp.zeros((nseq, p2), F32)
        ci = jnp.zeros((nseq, p2), F32)
        for c in range(nc):
            rows = pl.ds(base + c, nseq, stride=nc) if nseq > 1 else pl.ds(base + c, 1)
            xr_ref[rows, :] = cr
            xi_ref[rows, :] = ci
            sr, si = sr_ref[rows, :], si_ref[rows, :]
            cr, ci = sr + lam_r * cr - lam_i * ci, si + lam_r * ci + lam_i * cr
        cr = jnp.zeros((nseq, p2), F32)
        ci = jnp.zeros((nseq, p2), F32)
        for c in range(nc - 1, -1, -1):
            rows = pl.ds(base + c, nseq, stride=nc) if nseq > 1 else pl.ds(base + c, 1)
            xr_ref[rows, :] = jnp.where(fwd_lane, xr_ref[rows, :], cr)
            xi_ref[rows, :] = jnp.where(fwd_lane, xi_ref[rows, :], ci)
            sr, si = sr_ref[rows, :], si_ref[rows, :]
            cr, ci = sr + lam_r * cr - lam_i * ci, si + lam_r * ci + lam_i * cr

    y = jnp.dot(lhs, tz_ref[...], preferred_element_type=F32)
    x_carry = jnp.concatenate([xr_ref[...], xi_ref[...]], axis=-1).astype(BF16)
    y = y + jnp.dot(x_carry, wout_ref[0], preferred_element_type=F32)
    for o in range(gc):
        o_ref[o] = jax.nn.gelu(y[:, o * L:(o + 1) * L]).astype(o_ref.dtype)


def _ssm(u3, tables, segments):
    mvec, w_in, w_out, lam = tables
    ds, nck, L = u3.shape
    g_n = mvec.shape[0]
    gc = ds // g_n
    p2 = lam.shape[-1]
    body = functools.partial(_ssm_body, gc=gc, p2=p2, segments=segments)
    return pl.pallas_call(
        body,
        grid=(g_n,),
        in_specs=[pl.BlockSpec((gc, nck, L), lambda g: (g, 0, 0)),
                  pl.BlockSpec((1, gc * gc, 2 * L), lambda g: (g, 0, 0)),
                  pl.BlockSpec((1, gc * L, 2 * p2), lambda g: (g, 0, 0)),
                  pl.BlockSpec((1, 2 * p2, gc * L), lambda g: (g, 0, 0)),
                  pl.BlockSpec((1, 2, p2), lambda g: (g, 0, 0))],
        out_specs=pl.BlockSpec((gc, nck, L), lambda g: (g, 0, 0)),
        out_shape=jax.ShapeDtypeStruct((ds, nck, L), BF16),
        scratch_shapes=[pltpu.VMEM((gc * L, gc * L), BF16)] + [pltpu.VMEM((nck, p2), F32)] * 4,
        compiler_params=_params(1),
        name="ssm",
    )(u3, mvec, w_in, w_out, lam)


def _glu_body(x_ref, wa_ref, wb_ref, o_ref):
    x = x_ref[...]
    sw = o_ref.shape[1] // COL_SUBTILES
    for s in range(COL_SUBTILES):
        cols = slice(s * sw, (s + 1) * sw)
        a = jnp.dot(x, wa_ref[:, cols], preferred_element_type=F32)
        b = jnp.dot(x, wb_ref[:, cols], preferred_element_type=F32)
        o_ref[:, cols] = (a * jax.nn.sigmoid(b)).astype(o_ref.dtype)


def _glu(x, w, tm=1024, tn=512):
    m, k = x.shape
    n = w.shape[1] // 2
    tm, tn = min(tm, m), min(tn, n)
    assert m % tm == 0 and n % tn == 0
    nj = n // tn
    return pl.pallas_call(
        _glu_body,
        grid=(m // tm, nj),
        in_specs=[pl.BlockSpec((tm, k), lambda i, j: (i, 0)),
                  pl.BlockSpec((k, tn), lambda i, j: (0, j)),
                  pl.BlockSpec((k, tn), lambda i, j: (0, j + nj))],
        out_specs=pl.BlockSpec((tm, tn), lambda i, j: (i, j)),
        out_shape=jax.ShapeDtypeStruct((m, n), BF16),
        compiler_params=_params(2),
        name="glu",
    )(x, w, w)


def _mix_body(a_ref, w_ref, ga_ref, gb_ref, yb_ref, o_ref):
    a = a_ref[...]
    sw = o_ref.shape[1] // COL_SUBTILES
    for s in range(COL_SUBTILES):
        cols = slice(s * sw, (s + 1) * sw)
        ya = jnp.dot(a, w_ref[:, cols], preferred_element_type=F32)
        o_ref[:, cols] = (ga_ref[:, cols].astype(F32) * ya
                          + gb_ref[:, cols].astype(F32) * yb_ref[:, cols].astype(F32)
                          ).astype(o_ref.dtype)


def _mix(attn, w_ao, gates, y_b, tm=1024, tn=512):
    m, k = attn.shape
    n = w_ao.shape[1]
    tm, tn = min(tm, m), min(tn, n)
    assert m % tm == 0 and n % tn == 0
    ga0, gb0 = 0, n // tn
    return pl.pallas_call(
        _mix_body,
        grid=(m // tm, n // tn),
        in_specs=[pl.BlockSpec((tm, k), lambda i, j: (i, 0)),
                  pl.BlockSpec((k, tn), lambda i, j: (0, j)),
                  pl.BlockSpec((tm, tn), lambda i, j: (i, ga0 + j)),
                  pl.BlockSpec((tm, tn), lambda i, j: (i, gb0 + j)),
                  pl.BlockSpec((tm, tn), lambda i, j: (i, j))],
        out_specs=pl.BlockSpec((tm, tn), lambda i, j: (i, j)),
        out_shape=jax.ShapeDtypeStruct((m, n), BF16),
        compiler_params=_params(2),
        name="mix",
    )(attn, w_ao, gates, gates, y_b)


def _ln1_router_body(mixed_ref, wmix_ref, x_ref, g_ref, b_ref, wr_ref, br_ref, tri_ref,
                     x1_ref, info_ref, cnt_ref, run_ref, *, alpha, sub):
    i = pl.program_id(0)

    @pl.when(i == 0)
    def _():
        run_ref[...] = jnp.zeros_like(run_ref)

    run = run_ref[...]
    lane = lax.broadcasted_iota(jnp.int32, (sub, LANES), 1).astype(F32)
    for r in range(x_ref.shape[0] // sub):
        rows = slice(r * sub, (r + 1) * sub)
        z = alpha * x_ref[rows, :].astype(F32) + jnp.dot(mixed_ref[rows, :], wmix_ref[...],
                                                          preferred_element_type=F32)
        y = _layer_norm_rows(z, g_ref[...], b_ref[...])
        x1_ref[rows, :] = y
        y_hi = y.astype(BF16)
        y_lo = (y - y_hi.astype(F32)).astype(BF16)
        d_hi = jnp.dot(y_hi, wr_ref[...], preferred_element_type=F32)
        d_lo = jnp.dot(y_lo, wr_ref[:, 0:LANES], preferred_element_type=F32)
        logits = d_hi[:, 0:LANES] + d_hi[:, LANES:2 * LANES] + d_lo + br_ref[...]
        rem = logits
        idxs, vals = [], []
        for _ in range(TOP_K):
            mx = jnp.max(rem, axis=-1, keepdims=True)
            idx = jnp.min(jnp.where(rem == mx, lane, float(LANES)), axis=-1, keepdims=True)
            idxs.append(idx)
            vals.append(mx)
            rem = jnp.where(lane == idx, -jnp.inf, rem)
        exps = [jnp.exp(v - vals[0]) for v in vals]
        rden = 1.0 / (exps[0] + exps[1] + exps[2] + exps[3])
        onehot = jnp.zeros((sub, LANES), F32)
        for idx in idxs:
            onehot = onehot + jnp.where(lane == idx, 1.0, 0.0)
        prefix = jnp.dot(tri_ref[...], onehot.astype(BF16), preferred_element_type=F32) + run
        info = jnp.zeros((sub, LANES), F32)
        for k in range(TOP_K):
            rank = jnp.sum(jnp.where(lane == idxs[k], prefix, 0.0), axis=-1, keepdims=True)
            info = info + jnp.where(lane == float(k), idxs[k], 0.0)
            info = info + jnp.where(lane == float(TOP_K + k), exps[k] * rden, 0.0)
            info = info + jnp.where(lane == float(2 * TOP_K + k), rank, 0.0)
        info_ref[rows, :] = info
        run = run + jnp.sum(onehot, axis=0, keepdims=True)
    run_ref[...] = run
    cnt_ref[...] = run


def _ln1_router(mixed, w_mix, x, ln_g, ln_b, w_r, b_r, alpha, tm=512, sub=256):
    t, d = mixed.shape
    e = w_r.shape[1]
    tm = min(tm, t)
    sub = min(sub, tm)
    assert t % tm == 0 and tm % sub == 0 and e <= LANES
    wr_pad = jnp.zeros((d, LANES), F32).at[:, :e].set(w_r.astype(F32))
    wr_hi = wr_pad.astype(BF16)
    wr_lo = (wr_pad - wr_hi.astype(F32)).astype(BF16)
    wr_split = jnp.concatenate([wr_hi, wr_lo], axis=1)
    br_pad = jnp.full((1, LANES), MASK_VALUE, F32).at[0, :e].set(b_r.astype(F32))
    tri = jnp.tril(jnp.ones((sub, sub), BF16), -1)
    row = lambda v: v.astype(F32).reshape(1, d)
    const = lambda i: (0, 0)
    return pl.pallas_call(
        functools.partial(_ln1_router_body, alpha=alpha, sub=sub),
        grid=(t // tm,),
        in_specs=[pl.BlockSpec((tm, d), lambda i: (i, 0)),
                  pl.BlockSpec((d, d), const),
                  pl.BlockSpec((tm, d), lambda i: (i, 0)),
                  pl.BlockSpec((1, d), const), pl.BlockSpec((1, d), const),
                  pl.BlockSpec((d, 2 * LANES), const), pl.BlockSpec((1, LANES), const),
                  pl.BlockSpec((sub, sub), const)],
        out_specs=[pl.BlockSpec((tm, d), lambda i: (i, 0)),
                   pl.BlockSpec((tm, LANES), lambda i: (i, 0)),
                   pl.BlockSpec((1, LANES), const)],
        out_shape=[jax.ShapeDtypeStruct((t, d), F32),
                   jax.ShapeDtypeStruct((t, LANES), F32),
                   jax.ShapeDtypeStruct((1, LANES), F32)],
        scratch_shapes=[pltpu.VMEM((1, LANES), F32)],
        compiler_params=_params(1),
        name="ln1_router",
    )(mixed, w_mix, x, row(ln_g), row(ln_b), wr_split, br_pad, tri)


def _row_copy(src, src_row, dst, dst_row, sem):
    return pltpu.make_async_copy(src.at[pl.ds(src_row, 1), :], dst.at[pl.ds(dst_row, 1), :], sem)


def _dispatch_body(pos_ref, x_ref, xs_in_hbm, xs_hbm, pos_smem, sem_idx, sem_rows, *, tm):
    del xs_in_hbm
    idx_cp = pltpu.make_async_copy(pos_ref.at[0, 0], pos_smem, sem_idx)
    idx_cp.start()
    idx_cp.wait()

    def issue(r, carry):
        for k in range(TOP_K):
            _row_copy(x_ref, r, xs_hbm, pos_smem[k * tm + r], sem_rows).start(priority=k % 2)
        return carry

    lax.fori_loop(0, tm, issue, 0)
    for k in range(TOP_K):
        pltpu.make_async_copy(x_ref, xs_hbm.at[pl.ds(0, tm), :], sem_rows).wait()


def _dispatch(x1, pos_tiles, xs_buf, tm):
    t, d = x1.shape
    return pl.pallas_call(
        functools.partial(_dispatch_body, tm=tm),
        grid=(t // tm,),
        in_specs=[pl.BlockSpec((1, 1, TOP_K * tm), lambda i: (i, 0, 0)),
                  pl.BlockSpec((tm, d), lambda i: (i, 0)),
                  pl.BlockSpec(memory_space=pl.ANY)],
        out_specs=pl.BlockSpec(memory_space=pl.ANY),
        out_shape=jax.ShapeDtypeStruct(xs_buf.shape, F32),
        input_output_aliases={2: 0},
        scratch_shapes=[pltpu.SMEM((TOP_K * tm,), jnp.int32),
                        pltpu.SemaphoreType.DMA(()), pltpu.SemaphoreType.DMA(())],
        compiler_params=_params(1),
        name="moe_dispatch",
    )(pos_tiles, x1, xs_buf)


def _expert_body(be_ref, nv_ref, xs_ref, w1g_ref, w1u_ref, b1g_ref, b1u_ref, w2_ref, b2_ref, o_ref):
    b = pl.program_id(0)
    f = pl.program_id(1)
    nv = nv_ref[b]

    @pl.when(jnp.logical_and(nv == 0, f == 0))
    def _():
        o_ref[...] = jnp.zeros_like(o_ref)

    @pl.when(jnp.logical_and(nv > 0, f == 0))
    def _():
        o_ref[...] = jnp.broadcast_to(b2_ref[0], o_ref.shape)

    @pl.when(nv > 0)
    def _():
        sub = xs_ref.shape[0] // ROW_SUBTILES
        for r in range(ROW_SUBTILES):
            rows = slice(r * sub, (r + 1) * sub)
            row = lax.broadcasted_iota(jnp.int32, (sub, 1), 0) + r * sub
            x = jnp.where(row < nv, xs_ref[rows, :], 0.0).astype(BF16)
            hg = jnp.dot(x, w1g_ref[0], preferred_element_type=F32) + b1g_ref[0]
            hu = jnp.dot(x, w1u_ref[0], preferred_element_type=F32) + b1u_ref[0]
            gate = jnp.minimum(hg, SWIGLU_LIMIT)
            up = jnp.clip(hu, -SWIGLU_LIMIT, SWIGLU_LIMIT)
            act = (up + 1.0) * (gate * jax.nn.sigmoid(SWIGLU_ALPHA * gate))
            o_ref[rows, :] += jnp.dot(act.astype(BF16), w2_ref[0], preferred_element_type=F32)


def _experts(xs, block_e, block_nv, w1, b1, w2, b2, tm, tf=1024):
    n_slots, d = xs.shape
    e, ff = w2.shape[0], w2.shape[1]
    tf = min(tf, ff)
    nf = ff // tf
    assert ff % tf == 0 and n_slots % tm == 0
    b1r = b1.astype(F32).reshape(e, 1, 2 * ff)
    b2r = b2.astype(F32).reshape(e, 1, d)
    fi = lambda b, f, nv: jnp.where(nv[b] > 0, f, nf - 1)
    grid_spec = pltpu.PrefetchScalarGridSpec(
        num_scalar_prefetch=2,
        grid=(n_slots // tm, nf),
        in_specs=[pl.BlockSpec((tm, d), lambda b, f, be, nv: (b, 0)),
                  pl.BlockSpec((1, d, tf), lambda b, f, be, nv: (be[b], 0, fi(b, f, nv))),
                  pl.BlockSpec((1, d, tf), lambda b, f, be, nv: (be[b], 0, nf + fi(b, f, nv))),
                  pl.BlockSpec((1, 1, tf), lambda b, f, be, nv: (be[b], 0, fi(b, f, nv))),
                  pl.BlockSpec((1, 1, tf), lambda b, f, be, nv: (be[b], 0, nf + fi(b, f, nv))),
                  pl.BlockSpec((1, tf, d), lambda b, f, be, nv: (be[b], fi(b, f, nv), 0)),
                  pl.BlockSpec((1, 1, d), lambda b, f, be, nv: (be[b], 0, 0))],
        out_specs=pl.BlockSpec((tm, d), lambda b, f, be, nv: (b, 0)),
    )
    return pl.pallas_call(
        _expert_body,
        grid_spec=grid_spec,
        out_shape=jax.ShapeDtypeStruct((n_slots, d), F32),
        compiler_params=_params(2),
        name="moe_experts",
    )(block_e, block_nv, xs, w1, w1, b1r, b1r, w2, b2r)


def _combine_body(pos_ref, pos_next_ref, info_ref, x1_ref, g_ref, b_ref, ys_hbm, x2_ref, x2b_ref,
                  pos_smem, buf, sem_idx, sem_rows, *, tm, alpha):
    i = pl.program_id(0)
    n = pl.num_programs(0)
    slot = i % 2

    def start_gather(pos_vmem, dst_slot):
        idx_cp = pltpu.make_async_copy(pos_vmem.at[0, 0], pos_smem, sem_idx)
        idx_cp.start()
        idx_cp.wait()

        def issue(r8, carry):
            r0 = pl.multiple_of(r8 * SUBLANES, SUBLANES)
            for j in range(SUBLANES):
                for k in range(TOP_K):
                    _row_copy(ys_hbm, pos_smem[k * tm + r0 + j], buf.at[dst_slot], k * tm + r0 + j,
                              sem_rows.at[dst_slot]).start(priority=k % 2)
            return carry

        lax.fori_loop(0, tm // SUBLANES, issue, 0)

    @pl.when(i == 0)
    def _():
        start_gather(pos_ref, 0)

    @pl.when(i + 1 < n)
    def _():
        start_gather(pos_next_ref, 1 - slot)

    pltpu.make_async_copy(ys_hbm.at[pl.ds(0, TOP_K * tm), :], buf.at[slot], sem_rows.at[slot]).wait()

    info = info_ref[...]
    f = jnp.zeros(x1_ref.shape, F32)
    for k in range(TOP_K):
        f = f + info[:, TOP_K + k:TOP_K + k + 1] * buf[slot, k * tm:(k + 1) * tm, :]
    y = _layer_norm_rows(alpha * x1_ref[...] + f, g_ref[...], b_ref[...])
    x2_ref[...] = y
    x2b_ref[...] = y.astype(BF16)


def _combine_ln2(ys, pos_tiles, info, x1, ln_g, ln_b, alpha, tm):
    t, d = x1.shape
    nt = t // tm
    row = lambda v: v.astype(F32).reshape(1, d)
    const = lambda i: (0, 0)
    return pl.pallas_call(
        functools.partial(_combine_body, tm=tm, alpha=alpha),
        grid=(nt,),
        in_specs=[pl.BlockSpec((1, 1, TOP_K * tm), lambda i: (i, 0, 0)),
                  pl.BlockSpec((1, 1, TOP_K * tm), lambda i: (jnp.minimum(i + 1, nt - 1), 0, 0)),
                  pl.BlockSpec((tm, LANES), lambda i: (i, 0)),
                  pl.BlockSpec((tm, d), lambda i: (i, 0)),
                  pl.BlockSpec((1, d), const), pl.BlockSpec((1, d), const),
                  pl.BlockSpec(memory_space=pl.ANY)],
        out_specs=[pl.BlockSpec((tm, d), lambda i: (i, 0)),
                   pl.BlockSpec((tm, d), lambda i: (i, 0))],
        out_shape=[jax.ShapeDtypeStruct((t, d), F32), jax.ShapeDtypeStruct((t, d), BF16)],
        scratch_shapes=[pltpu.SMEM((TOP_K * tm,), jnp.int32),
                        pltpu.VMEM((2, TOP_K * tm, d), F32),
                        pltpu.SemaphoreType.DMA(()), pltpu.SemaphoreType.DMA((2,))],
        compiler_params=_params(1),
        name="moe_combine_ln2",
    )(pos_tiles, pos_tiles, info, x1, row(ln_g), row(ln_b), ys)


def _pos_tiles(pos, tm):
    t = pos.shape[0]
    return pos.reshape(t // tm, tm, TOP_K).transpose(0, 2, 1).reshape(t // tm, 1, TOP_K * tm)


def _route(info, counts, n_exp, block):
    t = info.shape[0]
    idx = info[:, 0:TOP_K].astype(jnp.int32)
    rank = info[:, 2 * TOP_K:3 * TOP_K].astype(jnp.int32)
    cnt = counts[0, :n_exp].astype(jnp.int32)
    padded = (cnt + block - 1) // block * block
    pends = jnp.cumsum(padded)
    pstarts = pends - padded
    experts = jnp.arange(n_exp, dtype=jnp.int32)
    table = lambda tab, ids: jnp.sum(jnp.where(ids[..., None] == experts, tab, 0), axis=-1)
    pos = table(pstarts, idx) + rank
    n_blocks = _num_slot_blocks(t, n_exp, block)
    bstart = jnp.arange(n_blocks, dtype=jnp.int32) * block
    block_e = jnp.minimum(jnp.sum((pends[None, :] <= bstart[:, None]).astype(jnp.int32), axis=1),
                          n_exp - 1)
    block_nv = jnp.clip(table(pstarts + cnt, block_e) - bstart, 0, block).astype(jnp.int32)
    return pos, block_e, block_nv


def _num_slot_blocks(t, n_exp, block):
    return (t * TOP_K) // block + n_exp


def _encoder_layer(x, xb, xs_buf, geom, w_in, sink, w_ao, ssm_tabs, w_glu, w_mix, ln1_g, ln1_b,
                   w_r, b_r, w1, b1, w2, b2, ln2_g, ln2_b, *, layer, alpha, dims, slot_block,
                   tok_tile):
    n_heads, n_kv, hd, d_ssm, d = dims
    t = x.shape[0]
    q_w, kv_w = n_heads * hd, n_kv * hd
    u_col0 = q_w + 2 * kv_w
    gate_col0 = u_col0 + d_ssm
    attn_geom, ssm_segments = geom

    proj = _in_proj(xb, w_in, 0, gate_col0, gate=False)
    gates = _in_proj(xb, w_in, gate_col0, 2 * d, gate=True)
    attn = _attention(proj, sink, _attn_bias(n_heads), n_heads, n_kv, hd, attn_geom)
    u3 = proj[:, u_col0:gate_col0].T.reshape(d_ssm, t // SSM_CHUNK, SSM_CHUNK)
    gy = _ssm(u3, ssm_tabs, ssm_segments).reshape(d_ssm, t).T
    y_b = _glu(gy, w_glu)
    mixed = _mix(attn, w_ao, gates, y_b)
    x1, info, counts = _ln1_router(mixed, w_mix, x, ln1_g, ln1_b, w_r, b_r, alpha)

    n_exp = w_r.shape[1]
    pos, block_e, block_nv = _route(info, counts, n_exp, slot_block)
    pos_t = _pos_tiles(pos, tok_tile)
    xs = _dispatch(x1, pos_t, xs_buf, tok_tile)
    ys = _experts(xs, block_e + layer * n_exp, block_nv, w1, b1, w2, b2, slot_block)
    x2, x2b = _combine_ln2(ys, pos_t, info, x1, ln2_g, ln2_b, alpha, tok_tile)
    return x2, x2b, xs


def kernel(x_prompt, x_sample, w_in, attn_sink, w_attn_out, ssm_a_re, ssm_a_im, ssm_log_dt, ssm_b_re, ssm_b_im, ssm_c_re, ssm_c_im, ssm_d, w_glu, w_mix_out, ln1_g, ln1_b, router_w, router_b, expert_w1, expert_b1, expert_w2, expert_b2, ln2_g, ln2_b):
    depth = w_in.shape[0]
    bp, sp, d = x_prompt.shape
    bs, ss, _ = x_sample.shape
    n_heads = attn_sink.shape[1]
    hd = d // n_heads
    d_ssm = ssm_a_re.shape[2] * ssm_b_re.shape[4]
    kv_w = (w_in.shape[2] - d - d_ssm - 2 * d) // 2
    dims = (n_heads, kv_w // hd, hd, d_ssm, d)
    alpha = (2.0 * depth) ** 0.25
    assert sp % ATTN_BLOCK == 0 and ss % ATTN_BLOCK == 0 and ATTN_BLOCK == SSM_CHUNK

    tp = bp * sp
    attn_geom = (tp // ATTN_BLOCK, sp // ATTN_BLOCK, ss // ATTN_BLOCK)
    ssm_segments = ((0, bp, sp // SSM_CHUNK), (tp // SSM_CHUNK, bs, ss // SSM_CHUNK))
    geom = (attn_geom, ssm_segments)

    x = jnp.concatenate([x_prompt.reshape(tp, d), x_sample.reshape(bs * ss, d)], axis=0)
    xb = x
    t = x.shape[0]
    slot_block, tok_tile = min(SLOT_BLOCK, t), min(TOKEN_TILE, t)
    n_slots = _num_slot_blocks(t, router_w.shape[2], slot_block) * slot_block
    xs_buf = jnp.zeros((n_slots, d), F32)
    col_scale = jnp.where(jnp.arange(w_in.shape[2]) < n_heads * hd, hd ** -0.5 * LOG2E, 1.0)
    w_in_b = (w_in * col_scale.astype(F32)).astype(BF16)
    n_exp, ff = expert_w2.shape[1], expert_w2.shape[2]
    w1_all = expert_w1.astype(BF16).reshape(depth * n_exp, d, 2 * ff)
    w2_all = expert_w2.astype(BF16).reshape(depth * n_exp, ff, d)
    b1_all = expert_b1.reshape(depth * n_exp, 2 * ff)
    b2_all = expert_b2.reshape(depth * n_exp, d)
    for l in range(depth):
        tabs = _ssm_tables(ssm_a_re[l], ssm_a_im[l], ssm_log_dt[l], ssm_b_re[l], ssm_b_im[l],
                           ssm_c_re[l], ssm_c_im[l], ssm_d[l])
        x, xb, xs_buf = _encoder_layer(
            x, xb, xs_buf, geom, w_in_b[l], attn_sink[l].astype(F32),
            w_attn_out[l].astype(BF16), tabs, w_glu[l].astype(BF16), w_mix_out[l].astype(BF16),
            ln1_g[l], ln1_b[l], router_w[l], router_b[l], w1_all, b1_all, w2_all, b2_all,
            ln2_g[l], ln2_b[l], layer=l, alpha=alpha, dims=dims,
            slot_block=slot_block, tok_tile=tok_tile)
    return (x[:tp].reshape(bp, sp, d), x[tp:].reshape(bs, ss, d))
```

```python
import functools
import math

import jax
import jax.numpy as jnp
from jax import lax
from jax.experimental import pallas as pl
from jax.experimental.pallas import tpu as pltpu

F32 = jnp.float32
BF16 = jnp.bfloat16

LANES = 128
SUBLANES = 8
ATTN_BLOCK = 128
SSM_CHUNK = 128
TOP_K = 4
LN_EPS = 1e-5
SWIGLU_ALPHA = 1.702
SWIGLU_LIMIT = 7.0
MASK_VALUE = -1e30
LOG2E = math.log2(math.e)
SLOT_BLOCK = 1024
TOKEN_TILE = 128
COL_SUBTILES = 2
VMEM_LIMIT_BYTES = 56 * 1024 * 1024


def _params(n_grid_dims):
    return pltpu.CompilerParams(dimension_semantics=("arbitrary",) * n_grid_dims,
                                vmem_limit_bytes=VMEM_LIMIT_BYTES)


def _pack_bf16_pairs(v):
    h = v.shape[1] // 2
    hi = lax.bitcast_convert_type(v[:, :h].astype(BF16).astype(F32), jnp.uint32)
    lo = lax.bitcast_convert_type(v[:, h:].astype(BF16).astype(F32), jnp.uint32)
    return hi | (lo >> 16)


def _unpack_bf16_pairs(p):
    a = lax.bitcast_convert_type(p & jnp.uint32(0xFFFF0000), F32)
    b = lax.bitcast_convert_type(p << 16, F32)
    return a, b


def _layer_norm_rows(z, g, b):
    mu = jnp.mean(z, axis=-1, keepdims=True)
    zc = z - mu
    var = jnp.mean(zc * zc, axis=-1, keepdims=True)
    return zc * lax.rsqrt(var + LN_EPS) * g + b


def _proj_body(x_ref, w_ref, o_ref, *, gate, n_sub):
    x = x_ref[...].astype(BF16)
    sw = w_ref.shape[1] // n_sub
    for s in range(n_sub):
        cols = slice(s * sw, (s + 1) * sw)
        acc = jnp.dot(x, w_ref[:, cols], preferred_element_type=F32)
        if gate:
            acc = jax.nn.sigmoid(acc)
        o_ref[:, cols] = acc.astype(o_ref.dtype)


def _in_proj(x, w, col0, n, gate, tm=1024, tn=1024, n_sub=COL_SUBTILES):
    m, k = x.shape
    tm, tn = min(tm, m), min(tn, n)
    assert m % tm == 0 and n % tn == 0 and col0 % tn == 0 and tn % (n_sub * LANES) == 0
    j0 = col0 // tn
    return pl.pallas_call(
        functools.partial(_proj_body, gate=gate, n_sub=n_sub),
        grid=(m // tm, n // tn),
        in_specs=[pl.BlockSpec((tm, k), lambda i, j: (i, 0)),
                  pl.BlockSpec((k, tn), lambda i, j: (0, j0 + j))],
        out_specs=pl.BlockSpec((tm, tn), lambda i, j: (i, j)),
        out_shape=jax.ShapeDtypeStruct((m, n), BF16),
        compiler_params=_params(2),
        name="in_proj_gates" if gate else "in_proj",
    )(x, w)


def _attn_body(sink_ref, q_ref, kp_ref, kc_ref, kn_ref, vp_ref, vc_ref, vn_ref, bias_ref, o_ref,
               *, n_kv, group, hd, geom):
    i = pl.program_id(0)
    n_blk_a, bps_a, bps_b = geom
    in_a = i < n_blk_a
    pos = jnp.where(in_a, i % bps_a, (i - n_blk_a) % bps_b)
    bps = jnp.where(in_a, bps_a, bps_b)
    first_pen = jnp.where(pos == 0, MASK_VALUE, 0.0).astype(F32)
    last_pen = jnp.where(pos == bps - 1, MASK_VALUE, 0.0).astype(F32)
    lane = lax.broadcasted_iota(jnp.int32, (1, 3 * ATTN_BLOCK), 1)
    pen = jnp.where(lane < ATTN_BLOCK, first_pen, jnp.where(lane >= 2 * ATTN_BLOCK, last_pen, 0.0))
    nq = ATTN_BLOCK
    for g in range(n_kv):
        cs = slice(g * hd, (g + 1) * hd)
        k3 = jnp.concatenate([kp_ref[:, cs], kc_ref[:, cs], kn_ref[:, cs]], axis=0)
        v3 = jnp.concatenate([vp_ref[:, cs], vc_ref[:, cs], vn_ref[:, cs]], axis=0)
        heads = range(g * group, (g + 1) * group)
        q4 = jnp.concatenate([q_ref[:, h * hd:(h + 1) * hd] for h in heads], axis=0)
        s4 = lax.dot_general(q4, k3, (((1,), (1,)), ((), ())), preferred_element_type=F32)
        ps, rdens = [], []
        for jj, h in enumerate(heads):
            s = s4[jj * nq:(jj + 1) * nq] + bias_ref[h] + pen
            sink = sink_ref[h] * LOG2E
            m = jnp.maximum(jnp.max(s, axis=-1, keepdims=True), sink)
            p = jnp.exp2(s - m)
            rdens.append(1.0 / (jnp.sum(p, axis=-1, keepdims=True) + jnp.exp2(sink - m)))
            ps.append(p.astype(BF16))
        o4 = jnp.dot(jnp.concatenate(ps, axis=0), v3, preferred_element_type=F32)
        for jj, h in enumerate(heads):
            o_ref[:, h * hd:(h + 1) * hd] = (o4[jj * nq:(jj + 1) * nq] * rdens[jj]).astype(o_ref.dtype)


def _attention(proj, sink, bias, n_heads, n_kv, hd, geom):
    t = proj.shape[0]
    nb = t // ATTN_BLOCK
    q_w, kv_w = n_heads * hd, n_kv * hd
    kblk, vblk = q_w // kv_w, q_w // kv_w + 1
    prev = lambda i: jnp.maximum(i - 1, 0)
    nxt = lambda i: jnp.minimum(i + 1, nb - 1)
    body = functools.partial(_attn_body, n_kv=n_kv, group=n_heads // n_kv, hd=hd, geom=geom)
    return pl.pallas_call(
        body,
        grid=(nb,),
        in_specs=[pl.BlockSpec(memory_space=pltpu.SMEM),
                  pl.BlockSpec((ATTN_BLOCK, q_w), lambda i: (i, 0)),
                  pl.BlockSpec((ATTN_BLOCK, kv_w), lambda i: (prev(i), kblk)),
                  pl.BlockSpec((ATTN_BLOCK, kv_w), lambda i: (i, kblk)),
                  pl.BlockSpec((ATTN_BLOCK, kv_w), lambda i: (nxt(i), kblk)),
                  pl.BlockSpec((ATTN_BLOCK, kv_w), lambda i: (prev(i), vblk)),
                  pl.BlockSpec((ATTN_BLOCK, kv_w), lambda i: (i, vblk)),
                  pl.BlockSpec((ATTN_BLOCK, kv_w), lambda i: (nxt(i), vblk)),
                  pl.BlockSpec((n_heads, ATTN_BLOCK, 3 * ATTN_BLOCK), lambda i: (0, 0, 0))],
        out_specs=pl.BlockSpec((ATTN_BLOCK, q_w), lambda i: (i, 0)),
        out_shape=jax.ShapeDtypeStruct((t, q_w), BF16),
        compiler_params=_params(1),
        name="attention",
    )(sink, proj, proj, proj, proj, proj, proj, proj, bias)


def _attn_bias(n_heads):
    slopes = jnp.exp2(-8.0 * jnp.arange(1, n_heads + 1, dtype=F32) / n_heads)
    rel = jnp.arange(ATTN_BLOCK)[:, None] - (jnp.arange(3 * ATTN_BLOCK)[None, :] - ATTN_BLOCK)
    dist = jnp.abs(rel).astype(F32)
    bias = -(slopes * LOG2E)[:, None, None] * dist[None]
    return jnp.where((jnp.abs(rel) <= ATTN_BLOCK)[None], bias, MASK_VALUE)


def _ssm_tables(a_re, a_im, log_dt, b_re, b_im, c_re, c_im, d_skip):
    L = SSM_CHUNK
    hp = lax.Precision.HIGHEST
    g_n, p_n, gc = a_re.shape[1], a_re.shape[2], b_re.shape[3]
    dt = jnp.exp(log_dt.astype(F32))[..., None]
    ar, ai = a_re.astype(F32), a_im.astype(F32)
    mag = jnp.exp(ar * dt)
    lr, li = mag * jnp.cos(ai * dt), mag * jnp.sin(ai * dt)
    den = ar * ar + ai * ai
    zr = ((lr - 1.0) * ar + li * ai) / den
    zi = (li * ar - (lr - 1.0) * ai) / den
    br, bi = b_re.astype(F32), b_im.astype(F32)
    bbr = zr[..., None] * br - zi[..., None] * bi
    bbi = zr[..., None] * bi + zi[..., None] * br
    tau = jnp.arange(L + 1, dtype=F32)[:, None, None, None]
    pmag = jnp.exp(tau * (ar * dt)[None])
    ang = tau * (ai * dt)[None]
    pr, pi = pmag * jnp.cos(ang), pmag * jnp.sin(ang)
    cr, ci = c_re.astype(F32), c_im.astype(F32)

    def lag_kernel(r):
        wr = pr[:L, r, :, :, None] * bbr[r][None] - pi[:L, r, :, :, None] * bbi[r][None]
        wi = pr[:L, r, :, :, None] * bbi[r][None] + pi[:L, r, :, :, None] * bbr[r][None]
        return (jnp.einsum('gop,tgpi->giot', cr, wr, precision=hp)
                - jnp.einsum('gop,tgpi->giot', ci, wi, precision=hp))

    kf, kb = lag_kernel(0), lag_kernel(1)
    eye = jnp.eye(gc, dtype=F32)
    center = kf[..., 0] + kb[..., 0] + eye[None] * d_skip.astype(F32).reshape(g_n, 1, gc)
    mvec = jnp.concatenate([jnp.zeros_like(center)[..., None], jnp.flip(kb[..., 1:], axis=-1),
                            center[..., None], kf[..., 1:]], axis=-1)
    mvec = mvec.reshape(g_n, gc * gc, 2 * L)

    pf_r, pf_i = jnp.flip(pr[:L, 0], axis=0), jnp.flip(pi[:L, 0], axis=0)
    pb_r, pb_i = pr[:L, 1], pi[:L, 1]

    def in_map(p_r, p_i, r):
        re = p_r[:, :, :, None] * bbr[r][None] - p_i[:, :, :, None] * bbi[r][None]
        im = p_r[:, :, :, None] * bbi[r][None] + p_i[:, :, :, None] * bbr[r][None]
        return jnp.transpose(re, (1, 3, 0, 2)), jnp.transpose(im, (1, 3, 0, 2))

    fr, fi = in_map(pf_r, pf_i, 0)
    rr, ri = in_map(pb_r, pb_i, 1)
    w_in = jnp.concatenate([fr, rr, fi, ri], axis=-1).reshape(g_n, gc * L, 4 * p_n)

    def out_map(p_r, p_i):
        a_r = cr[None] * p_r[:, :, None, :] - ci[None] * p_i[:, :, None, :]
        a_i = cr[None] * p_i[:, :, None, :] + ci[None] * p_r[:, :, None, :]
        return jnp.transpose(a_r, (1, 3, 2, 0)), jnp.transpose(-a_i, (1, 3, 2, 0))

    of_r, of_i = out_map(pr[1:L + 1, 0], pi[1:L + 1, 0])
    ob_r, ob_i = out_map(jnp.flip(pr[1:L + 1, 1], axis=0), jnp.flip(pi[1:L + 1, 1], axis=0))
    w_out = jnp.concatenate([of_r, ob_r, of_i, ob_i], axis=1).reshape(g_n, 4 * p_n, gc * L)
    lam = jnp.stack([jnp.concatenate([pr[L, 0], pr[L, 1]], axis=-1),
                     jnp.concatenate([pi[L, 0], pi[L, 1]], axis=-1)], axis=1)
    return mvec, w_in.astype(BF16), w_out.astype(BF16), lam


def _ssm_body(u_ref, mvec_ref, win_ref, wout_ref, lam_ref, o_ref, tz_ref, sr_ref, si_ref,
              xr_ref, xi_ref, *, gc, p2, segments):
    L = SSM_CHUNK
    nck = u_ref.shape[1]
    row = lax.broadcasted_iota(jnp.int32, (L, L), 0)
    col = lax.broadcasted_iota(jnp.int32, (L, L), 1)
    upper = col >= row

    def build(i, carry):
        for o in range(gc):
            vec = mvec_ref[0, pl.ds(i * gc + o, 1), :]
            lo = jnp.broadcast_to(vec[:, :L], (L, L))
            hi = jnp.broadcast_to(vec[:, L:], (L, L))
            r_lo = pltpu.roll(lo, 0, 1, stride=1, stride_axis=0)
            r_hi = pltpu.roll(hi, 0, 1, stride=1, stride_axis=0)
            blk = jnp.where(upper, r_hi, r_lo)
            tz_ref[pl.ds(pl.multiple_of(i * L, L), L), o * L:(o + 1) * L] = blk.astype(BF16)
        return carry

    lax.fori_loop(0, gc, build, 0)

    lhs = jnp.concatenate([u_ref[i] for i in range(gc)], axis=-1)
    s = jnp.dot(lhs, win_ref[0], preferred_element_type=F32)
    sr_ref[...] = s[:, 0:p2]
    si_ref[...] = s[:, p2:2 * p2]
    lam_r, lam_i = lam_ref[0, 0:1, :], lam_ref[0, 1:2, :]
    fwd_lane = lax.broadcasted_iota(jnp.int32, (1, p2), 1) < p2 // 2

    for base, nseq, nc in segments:
        cr = jnp.zeros((nseq, p2), F32)
        ci = jnp.zeros((nseq, p2), F32)
        for c in range(nc):
            rows = pl.ds(base + c, nseq, stride=nc) if nseq > 1 else pl.ds(base + c, 1)
            xr_ref[rows, :] = cr
            xi_ref[rows, :] = ci
            sr, si = sr_ref[rows, :], si_ref[rows, :]
            cr, ci = sr + lam_r * cr - lam_i * ci, si + lam_r * ci + lam_i * cr
        cr = jnp.zeros((nseq, p2), F32)
        ci = jnp.zeros((nseq, p2), F32)
        for c in range(nc - 1, -1, -1):
            rows = pl.ds(base + c, nseq, stride=nc) if nseq > 1 else pl.ds(base + c, 1)
            xr_ref[rows, :] = jnp.where(fwd_lane, xr_ref[rows, :], cr)
            xi_ref[rows, :] = jnp.where(fwd_lane, xi_ref[rows, :], ci)
            sr, si = sr_ref[rows, :], si_ref[rows, :]
            cr, ci = sr + lam_r * cr - lam_i * ci, si + lam_r * ci + lam_i * cr

    y = jnp.dot(lhs, tz_ref[...], preferred_element_type=F32)
    x_carry = jnp.concatenate([xr_ref[...], xi_ref[...]], axis=-1).astype(BF16)
    y = y + jnp.dot(x_carry, wout_ref[0], preferred_element_type=F32)
    for o in range(gc):
        o_ref[o] = jax.nn.gelu(y[:, o * L:(o + 1) * L]).astype(o_ref.dtype)


def _ssm(u3, tables, segments):
    mvec, w_in, w_out, lam = tables
    ds, nck, L = u3.shape
    g_n = mvec.shape[0]
    gc = ds // g_n
    p2 = lam.shape[-1]
    body = functools.partial(_ssm_body, gc=gc, p2=p2, segments=segments)
    return pl.pallas_call(
        body,
        grid=(g_n,),
        in_specs=[pl.BlockSpec((gc, nck, L), lambda g: (g, 0, 0)),
                  pl.BlockSpec((1, gc * gc, 2 * L), lambda g: (g, 0, 0)),
                  pl.BlockSpec((1, gc * L, 2 * p2), lambda g: (g, 0, 0)),
                  pl.BlockSpec((1, 2 * p2, gc * L), lambda g: (g, 0, 0)),
                  pl.BlockSpec((1, 2, p2), lambda g: (g, 0, 0))],
        out_specs=pl.BlockSpec((gc, nck, L), lambda g: (g, 0, 0)),
        out_shape=jax.ShapeDtypeStruct((ds, nck, L), BF16),
        scratch_shapes=[pltpu.VMEM((gc * L, gc * L), BF16)] + [pltpu.VMEM((nck, p2), F32)] * 4,
        compiler_params=_params(1),
        name="ssm",
    )(u3, mvec, w_in, w_out, lam)


def _glu_body(x_ref, wa_ref, wb_ref, o_ref):
    x = x_ref[...]
    sw = o_ref.shape[1] // COL_SUBTILES
    for s in range(COL_SUBTILES):
        cols = slice(s * sw, (s + 1) * sw)
        a = jnp.dot(x, wa_ref[:, cols], preferred_element_type=F32)
        b = jnp.dot(x, wb_ref[:, cols], preferred_element_type=F32)
        o_ref[:, cols] = (a * jax.nn.sigmoid(b)).astype(o_ref.dtype)


def _glu(x, w, tm=1024, tn=512):
    m, k = x.shape
    n = w.shape[1] // 2
    tm, tn = min(tm, m), min(tn, n)
    assert m % tm == 0 and n % tn == 0
    nj = n // tn
    return pl.pallas_call(
        _glu_body,
        grid=(m // tm, nj),
        in_specs=[pl.BlockSpec((tm, k), lambda i, j: (i, 0)),
                  pl.BlockSpec((k, tn), lambda i, j: (0, j)),
                  pl.BlockSpec((k, tn), lambda i, j: (0, j + nj))],
        out_specs=pl.BlockSpec((tm, tn), lambda i, j: (i, j)),
        out_shape=jax.ShapeDtypeStruct((m, n), BF16),
        compiler_params=_params(2),
        name="glu",
    )(x, w, w)


def _mix_body(a_ref, w_ref, ga_ref, gb_ref, yb_ref, o_ref):
    a = a_ref[...]
    sw = o_ref.shape[1] // COL_SUBTILES
    for s in range(COL_SUBTILES):
        cols = slice(s * sw, (s + 1) * sw)
        ya = jnp.dot(a, w_ref[:, cols], preferred_element_type=F32)
        o_ref[:, cols] = (ga_ref[:, cols].astype(F32) * ya
                          + gb_ref[:, cols].astype(F32) * yb_ref[:, cols].astype(F32)
                          ).astype(o_ref.dtype)


def _mix(attn, w_ao, gates, y_b, tm=1024, tn=512):
    m, k = attn.shape
    n = w_ao.shape[1]
    tm, tn = min(tm, m), min(tn, n)
    assert m % tm == 0 and n % tn == 0
    ga0, gb0 = 0, n // tn
    return pl.pallas_call(
        _mix_body,
        grid=(m // tm, n // tn),
        in_specs=[pl.BlockSpec((tm, k), lambda i, j: (i, 0)),
                  pl.BlockSpec((k, tn), lambda i, j: (0, j)),
                  pl.BlockSpec((tm, tn), lambda i, j: (i, ga0 + j)),
                  pl.BlockSpec((tm, tn), lambda i, j: (i, gb0 + j)),
                  pl.BlockSpec((tm, tn), lambda i, j: (i, j))],
        out_specs=pl.BlockSpec((tm, tn), lambda i, j: (i, j)),
        out_shape=jax.ShapeDtypeStruct((m, n), BF16),
        compiler_params=_params(2),
        name="mix",
    )(attn, w_ao, gates, gates, y_b)


def _ln1_router_body(mixed_ref, wmix_ref, x_ref, g_ref, b_ref, wr_ref, br_ref, tri_ref,
                     x1_ref, x1p_ref, info_ref, cnt_ref, run_ref, *, alpha, sub):
    i = pl.program_id(0)

    @pl.when(i == 0)
    def _():
        run_ref[...] = jnp.zeros_like(run_ref)

    run = run_ref[...]
    lane = lax.broadcasted_iota(jnp.int32, (sub, LANES), 1).astype(F32)
    for r in range(x_ref.shape[0] // sub):
        rows = slice(r * sub, (r + 1) * sub)
        z = alpha * x_ref[rows, :].astype(F32) + jnp.dot(mixed_ref[rows, :], wmix_ref[...],
                                                          preferred_element_type=F32)
        y = _layer_norm_rows(z, g_ref[...], b_ref[...])
        x1_ref[rows, :] = y
        x1p_ref[rows, :] = _pack_bf16_pairs(y)
        y_hi = y.astype(BF16)
        y_lo = (y - y_hi.astype(F32)).astype(BF16)
        d_hi = jnp.dot(y_hi, wr_ref[...], preferred_element_type=F32)
        d_lo = jnp.dot(y_lo, wr_ref[:, 0:LANES], preferred_element_type=F32)
        logits = d_hi[:, 0:LANES] + d_hi[:, LANES:2 * LANES] + d_lo + br_ref[...]
        rem = logits
        idxs, vals = [], []
        for _ in range(TOP_K):
            mx = jnp.max(rem, axis=-1, keepdims=True)
            idx = jnp.min(jnp.where(rem == mx, lane, float(LANES)), axis=-1, keepdims=True)
            idxs.append(idx)
            vals.append(mx)
            rem = jnp.where(lane == idx, -jnp.inf, rem)
        exps = [jnp.exp(v - vals[0]) for v in vals]
        rden = 1.0 / (exps[0] + exps[1] + exps[2] + exps[3])
        onehot = jnp.zeros((sub, LANES), F32)
        for idx in idxs:
            onehot = onehot + jnp.where(lane == idx, 1.0, 0.0)
        prefix = jnp.dot(tri_ref[...], onehot.astype(BF16), preferred_element_type=F32) + run
        info = jnp.zeros((sub, LANES), F32)
        for k in range(TOP_K):
            rank = jnp.sum(jnp.where(lane == idxs[k], prefix, 0.0), axis=-1, keepdims=True)
            info = info + jnp.where(lane == float(k), idxs[k], 0.0)
            info = info + jnp.where(lane == float(TOP_K + k), exps[k] * rden, 0.0)
            info = info + jnp.where(lane == float(2 * TOP_K + k), rank, 0.0)
        info_ref[rows, :] = info
        run = run + jnp.sum(onehot, axis=0, keepdims=True)
    run_ref[...] = run
    cnt_ref[...] = run


def _ln1_router(mixed, w_mix, x, ln_g, ln_b, w_r, b_r, alpha, tm=512, sub=256):
    t, d = mixed.shape
    e = w_r.shape[1]
    tm = min(tm, t)
    sub = min(sub, tm)
    assert t % tm == 0 and tm % sub == 0 and e <= LANES
    wr_pad = jnp.zeros((d, LANES), F32).at[:, :e].set(w_r.astype(F32))
    wr_hi = wr_pad.astype(BF16)
    wr_lo = (wr_pad - wr_hi.astype(F32)).astype(BF16)
    wr_split = jnp.concatenate([wr_hi, wr_lo], axis=1)
    br_pad = jnp.full((1, LANES), MASK_VALUE, F32).at[0, :e].set(b_r.astype(F32))
    tri = jnp.tril(jnp.ones((sub, sub), BF16), -1)
    row = lambda v: v.astype(F32).reshape(1, d)
    const = lambda i: (0, 0)
    return pl.pallas_call(
        functools.partial(_ln1_router_body, alpha=alpha, sub=sub),
        grid=(t // tm,),
        in_specs=[pl.BlockSpec((tm, d), lambda i: (i, 0)),
                  pl.BlockSpec((d, d), const),
                  pl.BlockSpec((tm, d), lambda i: (i, 0)),
                  pl.BlockSpec((1, d), const), pl.BlockSpec((1, d), const),
                  pl.BlockSpec((d, 2 * LANES), const), pl.BlockSpec((1, LANES), const),
                  pl.BlockSpec((sub, sub), const)],
        out_specs=[pl.BlockSpec((tm, d), lambda i: (i, 0)),
                   pl.BlockSpec((tm, d // 2), lambda i: (i, 0)),
                   pl.BlockSpec((tm, LANES), lambda i: (i, 0)),
                   pl.BlockSpec((1, LANES), const)],
        out_shape=[jax.ShapeDtypeStruct((t, d), F32),
                   jax.ShapeDtypeStruct((t, d // 2), jnp.uint32),
                   jax.ShapeDtypeStruct((t, LANES), F32),
                   jax.ShapeDtypeStruct((1, LANES), F32)],
        scratch_shapes=[pltpu.VMEM((1, LANES), F32)],
        compiler_params=_params(1),
        name="ln1_router",
    )(mixed, w_mix, x, row(ln_g), row(ln_b), wr_split, br_pad, tri)


def _row_copy(src, src_row, dst, dst_row, sem):
    return pltpu.make_async_copy(src.at[pl.ds(src_row, 1), :], dst.at[pl.ds(dst_row, 1), :], sem)


def _dispatch_body(pos_ref, x_ref, xs_in_hbm, xs_hbm, pos_smem, sem_idx, sem_rows, *, tm):
    del xs_in_hbm
    idx_cp = pltpu.make_async_copy(pos_ref.at[0, 0], pos_smem, sem_idx)
    idx_cp.start()
    idx_cp.wait()

    def issue(r, carry):
        for k in range(TOP_K):
            _row_copy(x_ref, r, xs_hbm, pos_smem[k * tm + r], sem_rows).start(priority=k % 2)
        return carry

    lax.fori_loop(0, tm, issue, 0)
    for k in range(TOP_K):
        pltpu.make_async_copy(x_ref, xs_hbm.at[pl.ds(0, tm), :], sem_rows).wait()


def _dispatch(x1, pos_tiles, xs_buf, tm):
    t, d = x1.shape
    return pl.pallas_call(
        functools.partial(_dispatch_body, tm=tm),
        grid=(t // tm,),
        in_specs=[pl.BlockSpec((1, 1, TOP_K * tm), lambda i: (i, 0, 0)),
                  pl.BlockSpec((tm, d), lambda i: (i, 0)),
                  pl.BlockSpec(memory_space=pl.ANY)],
        out_specs=pl.BlockSpec(memory_space=pl.ANY),
        out_shape=jax.ShapeDtypeStruct(xs_buf.shape, xs_buf.dtype),
        input_output_aliases={2: 0},
        scratch_shapes=[pltpu.SMEM((TOP_K * tm,), jnp.int32),
                        pltpu.SemaphoreType.DMA(()), pltpu.SemaphoreType.DMA(())],
        compiler_params=_params(1),
        name="moe_dispatch",
    )(pos_tiles, x1, xs_buf)


def _expert_body(be_ref, nv_ref, xs_ref, w1g_ref, w1u_ref, b1g_ref, b1u_ref, w2_ref, b2_ref, o_ref,
                 acc_ref, *, sub):
    b = pl.program_id(0)
    f = pl.program_id(1)
    last = pl.num_programs(1) - 1
    nv = nv_ref[b]
    for r in range(xs_ref.shape[0] // sub):
        rows = slice(r * sub, (r + 1) * sub)

        @pl.when(jnp.logical_and(nv > r * sub, f == 0))
        def _():
            acc_ref[rows, :] = jnp.broadcast_to(b2_ref[0], (sub, acc_ref.shape[1]))

        @pl.when(nv > r * sub)
        def _():
            row = lax.broadcasted_iota(jnp.int32, (sub, 1), 0) + r * sub
            xa, xb = _unpack_bf16_pairs(jnp.where(row < nv, xs_ref[rows, :], jnp.uint32(0)))
            x = jnp.concatenate([xa, xb], axis=1).astype(BF16)
            hg = jnp.dot(x, w1g_ref[0], preferred_element_type=F32) + b1g_ref[0]
            hu = jnp.dot(x, w1u_ref[0], preferred_element_type=F32) + b1u_ref[0]
            gate = jnp.minimum(hg, SWIGLU_LIMIT)
            up = jnp.clip(hu, -SWIGLU_LIMIT, SWIGLU_LIMIT)
            act = (up + 1.0) * (gate * jax.nn.sigmoid(SWIGLU_ALPHA * gate))
            y = acc_ref[rows, :] + jnp.dot(act.astype(BF16), w2_ref[0], preferred_element_type=F32)

            @pl.when(f < last)
            def _():
                acc_ref[rows, :] = y

            @pl.when(f == last)
            def _():
                o_ref[rows, :] = _pack_bf16_pairs(y)

        @pl.when(jnp.logical_and(nv <= r * sub, f == last))
        def _():
            o_ref[rows, :] = jnp.zeros((sub, o_ref.shape[1]), o_ref.dtype)


def _experts(xs, block_e, block_nv, w1, b1, w2, b2, tm, tf=1024, sub=256):
    n_slots, dh = xs.shape
    d = 2 * dh
    e, ff = w2.shape[0], w2.shape[1]
    tf = min(tf, ff)
    sub = min(sub, tm)
    nf = ff // tf
    assert ff % tf == 0 and n_slots % tm == 0 and tm % sub == 0
    b1r = b1.astype(F32).reshape(e, 1, 2 * ff)
    b2r = b2.astype(F32).reshape(e, 1, d)
    fi = lambda b, f, nv: jnp.where(nv[b] > 0, f, nf - 1)
    grid_spec = pltpu.PrefetchScalarGridSpec(
        num_scalar_prefetch=2,
        grid=(n_slots // tm, nf),
        in_specs=[pl.BlockSpec((tm, dh), lambda b, f, be, nv: (b, 0)),
                  pl.BlockSpec((1, d, tf), lambda b, f, be, nv: (be[b], 0, fi(b, f, nv))),
                  pl.BlockSpec((1, d, tf), lambda b, f, be, nv: (be[b], 0, nf + fi(b, f, nv))),
                  pl.BlockSpec((1, 1, tf), lambda b, f, be, nv: (be[b], 0, fi(b, f, nv))),
                  pl.BlockSpec((1, 1, tf), lambda b, f, be, nv: (be[b], 0, nf + fi(b, f, nv))),
                  pl.BlockSpec((1, tf, d), lambda b, f, be, nv: (be[b], fi(b, f, nv), 0)),
                  pl.BlockSpec((1, 1, d), lambda b, f, be, nv: (be[b], 0, 0))],
        out_specs=pl.BlockSpec((tm, dh), lambda b, f, be, nv: (b, 0)),
        scratch_shapes=[pltpu.VMEM((tm, d), F32)],
    )
    return pl.pallas_call(
        functools.partial(_expert_body, sub=sub),
        grid_spec=grid_spec,
        out_shape=jax.ShapeDtypeStruct((n_slots, dh), jnp.uint32),
        compiler_params=_params(2),
        name="moe_experts",
    )(block_e, block_nv, xs, w1, w1, b1r, b1r, w2, b2r)


def _combine_body(pos_ref, pos_next_ref, info_ref, x1_ref, g_ref, b_ref, ys_hbm, x2_ref, x2b_ref,
                  pos_smem, buf, sem_idx, sem_rows, *, tm, alpha):
    i = pl.program_id(0)
    n = pl.num_programs(0)
    slot = i % 2

    def start_gather(pos_vmem, dst_slot):
        idx_cp = pltpu.make_async_copy(pos_vmem.at[0, 0], pos_smem, sem_idx)
        idx_cp.start()
        idx_cp.wait()

        def issue(r8, carry):
            r0 = pl.multiple_of(r8 * SUBLANES, SUBLANES)
            for j in range(SUBLANES):
                for k in range(TOP_K):
                    _row_copy(ys_hbm, pos_smem[k * tm + r0 + j], buf.at[dst_slot], k * tm + r0 + j,
                              sem_rows.at[dst_slot]).start(priority=k % 2)
            return carry

        lax.fori_loop(0, tm // SUBLANES, issue, 0)

    @pl.when(i == 0)
    def _():
        start_gather(pos_ref, 0)

    @pl.when(i + 1 < n)
    def _():
        start_gather(pos_next_ref, 1 - slot)

    pltpu.make_async_copy(ys_hbm.at[pl.ds(0, TOP_K * tm), :], buf.at[slot], sem_rows.at[slot]).wait()

    info = info_ref[...]
    dh = buf.shape[2]
    fa = jnp.zeros((tm, dh), F32)
    fb = jnp.zeros((tm, dh), F32)
    for k in range(TOP_K):
        gate = info[:, TOP_K + k:TOP_K + k + 1]
        ya, yb = _unpack_bf16_pairs(buf[slot, k * tm:(k + 1) * tm, :])
        fa = fa + gate * ya
        fb = fb + gate * yb
    f = jnp.concatenate([fa, fb], axis=1)
    y = _layer_norm_rows(alpha * x1_ref[...] + f, g_ref[...], b_ref[...])
    x2_ref[...] = y
    x2b_ref[...] = y.astype(BF16)


def _combine_ln2(ys, pos_tiles, info, x1, ln_g, ln_b, alpha, tm):
    t, d = x1.shape
    nt = t // tm
    row = lambda v: v.astype(F32).reshape(1, d)
    const = lambda i: (0, 0)
    return pl.pallas_call(
        functools.partial(_combine_body, tm=tm, alpha=alpha),
        grid=(nt,),
        in_specs=[pl.BlockSpec((1, 1, TOP_K * tm), lambda i: (i, 0, 0)),
                  pl.BlockSpec((1, 1, TOP_K * tm), lambda i: (jnp.minimum(i + 1, nt - 1), 0, 0)),
                  pl.BlockSpec((tm, LANES), lambda i: (i, 0)),
                  pl.BlockSpec((tm, d), lambda i: (i, 0)),
                  pl.BlockSpec((1, d), const), pl.BlockSpec((1, d), const),
                  pl.BlockSpec(memory_space=pl.ANY)],
        out_specs=[pl.BlockSpec((tm, d), lambda i: (i, 0)),
                   pl.BlockSpec((tm, d), lambda i: (i, 0))],
        out_shape=[jax.ShapeDtypeStruct((t, d), F32), jax.ShapeDtypeStruct((t, d), BF16)],
        scratch_shapes=[pltpu.SMEM((TOP_K * tm,), jnp.int32),
                        pltpu.VMEM((2, TOP_K * tm, d // 2), jnp.uint32),
                        pltpu.SemaphoreType.DMA(()), pltpu.SemaphoreType.DMA((2,))],
        compiler_params=_params(1),
        name="moe_combine_ln2",
    )(pos_tiles, pos_tiles, info, x1, row(ln_g), row(ln_b), ys)


def _pos_tiles(pos, tm):
    t = pos.shape[0]
    return pos.reshape(t // tm, tm, TOP_K).transpose(0, 2, 1).reshape(t // tm, 1, TOP_K * tm)


def _route(info, counts, n_exp, block):
    t = info.shape[0]
    idx = info[:, 0:TOP_K].astype(jnp.int32)
    rank = info[:, 2 * TOP_K:3 * TOP_K].astype(jnp.int32)
    cnt = counts[0, :n_exp].astype(jnp.int32)
    padded = (cnt + block - 1) // block * block
    pends = jnp.cumsum(padded)
    pstarts = pends - padded
    experts = jnp.arange(n_exp, dtype=jnp.int32)
    table = lambda tab, ids: jnp.sum(jnp.where(ids[..., None] == experts, tab, 0), axis=-1)
    pos = table(pstarts, idx) + rank
    n_blocks = _num_slot_blocks(t, n_exp, block)
    bstart = jnp.arange(n_blocks, dtype=jnp.int32) * block
    block_e = jnp.minimum(jnp.sum((pends[None, :] <= bstart[:, None]).astype(jnp.int32), axis=1),
                          n_exp - 1)
    block_nv = jnp.clip(table(pstarts + cnt, block_e) - bstart, 0, block).astype(jnp.int32)
    return pos, block_e, block_nv


def _num_slot_blocks(t, n_exp, block):
    return (t * TOP_K) // block + n_exp


def _encoder_layer(x, xb, xs_buf, geom, w_in, sink, w_ao, ssm_tabs, w_glu, w_mix, ln1_g, ln1_b,
                   w_r, b_r, w1, b1, w2, b2, ln2_g, ln2_b, *, layer, alpha, dims, slot_block,
                   tok_tile):
    n_heads, n_kv, hd, d_ssm, d = dims
    t = x.shape[0]
    q_w, kv_w = n_heads * hd, n_kv * hd
    u_col0 = q_w + 2 * kv_w
    gate_col0 = u_col0 + d_ssm
    attn_geom, ssm_segments = geom

    proj = _in_proj(xb, w_in, 0, gate_col0, gate=False)
    gates = _in_proj(xb, w_in, gate_col0, 2 * d, gate=True)
    attn = _attention(proj, sink, _attn_bias(n_heads), n_heads, n_kv, hd, attn_geom)
    u3 = proj[:, u_col0:gate_col0].T.reshape(d_ssm, t // SSM_CHUNK, SSM_CHUNK)
    gy = _ssm(u3, ssm_tabs, ssm_segments).reshape(d_ssm, t).T
    y_b = _glu(gy, w_glu)
    mixed = _mix(attn, w_ao, gates, y_b)
    x1, x1p, info, counts = _ln1_router(mixed, w_mix, x, ln1_g, ln1_b, w_r, b_r, alpha)

    n_exp = w_r.shape[1]
    pos, block_e, block_nv = _route(info, counts, n_exp, slot_block)
    pos_t = _pos_tiles(pos, tok_tile)
    xs = _dispatch(x1p, pos_t, xs_buf, tok_tile)
    ys = _experts(xs, block_e + layer * n_exp, block_nv, w1, b1, w2, b2, slot_block)
    x2, x2b = _combine_ln2(ys, pos_t, info, x1, ln2_g, ln2_b, alpha, tok_tile)
    return x2, x2b, xs


def kernel(x_prompt, x_sample, w_in, attn_sink, w_attn_out, ssm_a_re, ssm_a_im, ssm_log_dt, ssm_b_re, ssm_b_im, ssm_c_re, ssm_c_im, ssm_d, w_glu, w_mix_out, ln1_g, ln1_b, router_w, router_b, expert_w1, expert_b1, expert_w2, expert_b2, ln2_g, ln2_b):
    depth = w_in.shape[0]
    bp, sp, d = x_prompt.shape
    bs, ss, _ = x_sample.shape
    n_heads = attn_sink.shape[1]
    hd = d // n_heads
    d_ssm = ssm_a_re.shape[2] * ssm_b_re.shape[4]
    kv_w = (w_in.shape[2] - d - d_ssm - 2 * d) // 2
    dims = (n_heads, kv_w // hd, hd, d_ssm, d)
    alpha = (2.0 * depth) ** 0.25
    assert sp % ATTN_BLOCK == 0 and ss % ATTN_BLOCK == 0 and ATTN_BLOCK == SSM_CHUNK

    tp = bp * sp
    attn_geom = (tp // ATTN_BLOCK, sp // ATTN_BLOCK, ss // ATTN_BLOCK)
    ssm_segments = ((0, bp, sp // SSM_CHUNK), (tp // SSM_CHUNK, bs, ss // SSM_CHUNK))
    geom = (attn_geom, ssm_segments)

    x = jnp.concatenate([x_prompt.reshape(tp, d), x_sample.reshape(bs * ss, d)], axis=0)
    xb = x
    t = x.shape[0]
    slot_block, tok_tile = min(SLOT_BLOCK, t), min(TOKEN_TILE, t)
    n_slots = _num_slot_blocks(t, router_w.shape[2], slot_block) * slot_block
    xs_buf = jnp.zeros((n_slots, d // 2), jnp.uint32)
    col_scale = jnp.where(jnp.arange(w_in.shape[2]) < n_heads * hd, hd ** -0.5 * LOG2E, 1.0)
    w_in_b = (w_in * col_scale.astype(F32)).astype(BF16)
    n_exp, ff = expert_w2.shape[1], expert_w2.shape[2]
    w1_all = expert_w1.astype(BF16).reshape(depth * n_exp, d, 2 * ff)
    w2_all = expert_w2.astype(BF16).reshape(depth * n_exp, ff, d)
    b1_all = expert_b1.reshape(depth * n_exp, 2 * ff)
    b2_all = expert_b2.reshape(depth * n_exp, d)
    for l in range(depth):
        tabs = _ssm_tables(ssm_a_re[l], ssm_a_im[l], ssm_log_dt[l], ssm_b_re[l], ssm_b_im[l],
                           ssm_c_re[l], ssm_c_im[l], ssm_d[l])
        x, xb, xs_buf = _encoder_layer(
            x, xb, xs_buf, geom, w_in_b[l], attn_sink[l].astype(F32),
            w_attn_out[l].astype(BF16), tabs, w_glu[l].astype(BF16), w_mix_out[l].astype(BF16),
            ln1_g[l], ln1_b[l], router_w[l], router_b[l], w1_all, b1_all, w2_all, b2_all,
            ln2_g[l], ln2_b[l], layer=l, alpha=alpha, dims=dims,
            slot_block=slot_block, tok_tile=tok_tile)
    return (x[:tp].reshape(bp, sp, d), x[tp:].reshape(bs, ss, d))
```

```python
import functools
import math

import jax
import jax.numpy as jnp
from jax import lax
from jax.experimental import pallas as pl
from jax.experimental.pallas import tpu as pltpu

F32 = jnp.float32
BF16 = jnp.bfloat16

LANES = 128
SUBLANES = 8
ATTN_BLOCK = 128
SSM_CHUNK = 128
TOP_K = 4
LN_EPS = 1e-5
SWIGLU_ALPHA = 1.702
SWIGLU_LIMIT = 7.0
MASK_VALUE = -1e30
LOG2E = math.log2(math.e)
SLOT_BLOCK = 1024
TOKEN_TILE = 512
COL_SUBTILES = 2
VMEM_LIMIT_BYTES = 56 * 1024 * 1024


def _params(n_grid_dims):
    return pltpu.CompilerParams(dimension_semantics=("arbitrary",) * n_grid_dims,
                                vmem_limit_bytes=VMEM_LIMIT_BYTES)


def _pack_bf16_pairs(v):
    h = v.shape[1] // 2
    hi = lax.bitcast_convert_type(v[:, :h].astype(BF16).astype(F32), jnp.uint32)
    lo = lax.bitcast_convert_type(v[:, h:].astype(BF16).astype(F32), jnp.uint32)
    return hi | (lo >> 16)


def _unpack_bf16_pairs(p):
    a = lax.bitcast_convert_type(p & jnp.uint32(0xFFFF0000), F32)
    b = lax.bitcast_convert_type(p << 16, F32)
    return a, b


def _layer_norm_rows(z, g, b):
    mu = jnp.mean(z, axis=-1, keepdims=True)
    zc = z - mu
    var = jnp.mean(zc * zc, axis=-1, keepdims=True)
    return zc * lax.rsqrt(var + LN_EPS) * g + b


def _proj_body(x_ref, w_ref, o_ref, *, gate, n_sub):
    x = x_ref[...].astype(BF16)
    sw = w_ref.shape[1] // n_sub
    for s in range(n_sub):
        cols = slice(s * sw, (s + 1) * sw)
        acc = jnp.dot(x, w_ref[:, cols], preferred_element_type=F32)
        if gate:
            acc = jax.nn.sigmoid(acc)
        o_ref[:, cols] = acc.astype(o_ref.dtype)


def _in_proj(x, w, col0, n, gate, tm=1024, tn=1024, n_sub=COL_SUBTILES):
    m, k = x.shape
    tm, tn = min(tm, m), min(tn, n)
    assert m % tm == 0 and n % tn == 0 and col0 % tn == 0 and tn % (n_sub * LANES) == 0
    j0 = col0 // tn
    return pl.pallas_call(
        functools.partial(_proj_body, gate=gate, n_sub=n_sub),
        grid=(m // tm, n // tn),
        in_specs=[pl.BlockSpec((tm, k), lambda i, j: (i, 0)),
                  pl.BlockSpec((k, tn), lambda i, j: (0, j0 + j))],
        out_specs=pl.BlockSpec((tm, tn), lambda i, j: (i, j)),
        out_shape=jax.ShapeDtypeStruct((m, n), BF16),
        compiler_params=_params(2),
        name="in_proj_gates" if gate else "in_proj",
    )(x, w)


def _attn_body(sink_ref, q_ref, kp_ref, kc_ref, kn_ref, vp_ref, vc_ref, vn_ref, bias_ref, o_ref,
               *, n_kv, group, hd, geom):
    i = pl.program_id(0)
    n_blk_a, bps_a, bps_b = geom
    in_a = i < n_blk_a
    pos = jnp.where(in_a, i % bps_a, (i - n_blk_a) % bps_b)
    bps = jnp.where(in_a, bps_a, bps_b)
    first_pen = jnp.where(pos == 0, MASK_VALUE, 0.0).astype(F32)
    last_pen = jnp.where(pos == bps - 1, MASK_VALUE, 0.0).astype(F32)
    lane = lax.broadcasted_iota(jnp.int32, (1, 3 * ATTN_BLOCK), 1)
    pen = jnp.where(lane < ATTN_BLOCK, first_pen, jnp.where(lane >= 2 * ATTN_BLOCK, last_pen, 0.0))
    nq = ATTN_BLOCK
    for g in range(n_kv):
        cs = slice(g * hd, (g + 1) * hd)
        k3 = jnp.concatenate([kp_ref[:, cs], kc_ref[:, cs], kn_ref[:, cs]], axis=0)
        v3 = jnp.concatenate([vp_ref[:, cs], vc_ref[:, cs], vn_ref[:, cs]], axis=0)
        heads = range(g * group, (g + 1) * group)
        q4 = jnp.concatenate([q_ref[:, h * hd:(h + 1) * hd] for h in heads], axis=0)
        s4 = lax.dot_general(q4, k3, (((1,), (1,)), ((), ())), preferred_element_type=F32)
        ps, rdens = [], []
        for jj, h in enumerate(heads):
            s = s4[jj * nq:(jj + 1) * nq] + bias_ref[h] + pen
            sink = sink_ref[h] * LOG2E
            m = jnp.maximum(jnp.max(s, axis=-1, keepdims=True), sink)
            p = jnp.exp2(s - m)
            rdens.append(1.0 / (jnp.sum(p, axis=-1, keepdims=True) + jnp.exp2(sink - m)))
            ps.append(p.astype(BF16))
        o4 = jnp.dot(jnp.concatenate(ps, axis=0), v3, preferred_element_type=F32)
        for jj, h in enumerate(heads):
            o_ref[:, h * hd:(h + 1) * hd] = (o4[jj * nq:(jj + 1) * nq] * rdens[jj]).astype(o_ref.dtype)


def _attention(proj, sink, bias, n_heads, n_kv, hd, geom):
    t = proj.shape[0]
    nb = t // ATTN_BLOCK
    q_w, kv_w = n_heads * hd, n_kv * hd
    kblk, vblk = q_w // kv_w, q_w // kv_w + 1
    prev = lambda i: jnp.maximum(i - 1, 0)
    nxt = lambda i: jnp.minimum(i + 1, nb - 1)
    body = functools.partial(_attn_body, n_kv=n_kv, group=n_heads // n_kv, hd=hd, geom=geom)
    return pl.pallas_call(
        body,
        grid=(nb,),
        in_specs=[pl.BlockSpec(memory_space=pltpu.SMEM),
                  pl.BlockSpec((ATTN_BLOCK, q_w), lambda i: (i, 0)),
                  pl.BlockSpec((ATTN_BLOCK, kv_w), lambda i: (prev(i), kblk)),
                  pl.BlockSpec((ATTN_BLOCK, kv_w), lambda i: (i, kblk)),
                  pl.BlockSpec((ATTN_BLOCK, kv_w), lambda i: (nxt(i), kblk)),
                  pl.BlockSpec((ATTN_BLOCK, kv_w), lambda i: (prev(i), vblk)),
                  pl.BlockSpec((ATTN_BLOCK, kv_w), lambda i: (i, vblk)),
                  pl.BlockSpec((ATTN_BLOCK, kv_w), lambda i: (nxt(i), vblk)),
                  pl.BlockSpec((n_heads, ATTN_BLOCK, 3 * ATTN_BLOCK), lambda i: (0, 0, 0))],
        out_specs=pl.BlockSpec((ATTN_BLOCK, q_w), lambda i: (i, 0)),
        out_shape=jax.ShapeDtypeStruct((t, q_w), BF16),
        compiler_params=_params(1),
        name="attention",
    )(sink, proj, proj, proj, proj, proj, proj, proj, bias)


def _attn_bias(n_heads):
    slopes = jnp.exp2(-8.0 * jnp.arange(1, n_heads + 1, dtype=F32) / n_heads)
    rel = jnp.arange(ATTN_BLOCK)[:, None] - (jnp.arange(3 * ATTN_BLOCK)[None, :] - ATTN_BLOCK)
    dist = jnp.abs(rel).astype(F32)
    bias = -(slopes * LOG2E)[:, None, None] * dist[None]
    return jnp.where((jnp.abs(rel) <= ATTN_BLOCK)[None], bias, MASK_VALUE)


def _ssm_tables(a_re, a_im, log_dt, b_re, b_im, c_re, c_im, d_skip):
    L = SSM_CHUNK
    hp = lax.Precision.HIGHEST
    g_n, p_n, gc = a_re.shape[1], a_re.shape[2], b_re.shape[3]
    dt = jnp.exp(log_dt.astype(F32))[..., None]
    ar, ai = a_re.astype(F32), a_im.astype(F32)
    mag = jnp.exp(ar * dt)
    lr, li = mag * jnp.cos(ai * dt), mag * jnp.sin(ai * dt)
    den = ar * ar + ai * ai
    zr = ((lr - 1.0) * ar + li * ai) / den
    zi = (li * ar - (lr - 1.0) * ai) / den
    br, bi = b_re.astype(F32), b_im.astype(F32)
    bbr = zr[..., None] * br - zi[..., None] * bi
    bbi = zr[..., None] * bi + zi[..., None] * br
    tau = jnp.arange(L + 1, dtype=F32)[:, None, None, None]
    pmag = jnp.exp(tau * (ar * dt)[None])
    ang = tau * (ai * dt)[None]
    pr, pi = pmag * jnp.cos(ang), pmag * jnp.sin(ang)
    cr, ci = c_re.astype(F32), c_im.astype(F32)

    def lag_kernel(r):
        wr = pr[:L, r, :, :, None] * bbr[r][None] - pi[:L, r, :, :, None] * bbi[r][None]
        wi = pr[:L, r, :, :, None] * bbi[r][None] + pi[:L, r, :, :, None] * bbr[r][None]
        return (jnp.einsum('gop,tgpi->giot', cr, wr, precision=hp)
                - jnp.einsum('gop,tgpi->giot', ci, wi, precision=hp))

    kf, kb = lag_kernel(0), lag_kernel(1)
    eye = jnp.eye(gc, dtype=F32)
    center = kf[..., 0] + kb[..., 0] + eye[None] * d_skip.astype(F32).reshape(g_n, 1, gc)
    mvec = jnp.concatenate([jnp.zeros_like(center)[..., None], jnp.flip(kb[..., 1:], axis=-1),
                            center[..., None], kf[..., 1:]], axis=-1)
    mvec = mvec.reshape(g_n, gc * gc, 2 * L)

    pf_r, pf_i = jnp.flip(pr[:L, 0], axis=0), jnp.flip(pi[:L, 0], axis=0)
    pb_r, pb_i = pr[:L, 1], pi[:L, 1]

    def in_map(p_r, p_i, r):
        re = p_r[:, :, :, None] * bbr[r][None] - p_i[:, :, :, None] * bbi[r][None]
        im = p_r[:, :, :, None] * bbi[r][None] + p_i[:, :, :, None] * bbr[r][None]
        return jnp.transpose(re, (1, 3, 0, 2)), jnp.transpose(im, (1, 3, 0, 2))

    fr, fi = in_map(pf_r, pf_i, 0)
    rr, ri = in_map(pb_r, pb_i, 1)
    w_in = jnp.concatenate([fr, rr, fi, ri], axis=-1).reshape(g_n, gc * L, 4 * p_n)

    def out_map(p_r, p_i):
        a_r = cr[None] * p_r[:, :, None, :] - ci[None] * p_i[:, :, None, :]
        a_i = cr[None] * p_i[:, :, None, :] + ci[None] * p_r[:, :, None, :]
        return jnp.transpose(a_r, (1, 3, 2, 0)), jnp.transpose(-a_i, (1, 3, 2, 0))

    of_r, of_i = out_map(pr[1:L + 1, 0], pi[1:L + 1, 0])
    ob_r, ob_i = out_map(jnp.flip(pr[1:L + 1, 1], axis=0), jnp.flip(pi[1:L + 1, 1], axis=0))
    w_out = jnp.concatenate([of_r, ob_r, of_i, ob_i], axis=1).reshape(g_n, 4 * p_n, gc * L)
    lam = jnp.stack([jnp.concatenate([pr[L, 0], pr[L, 1]], axis=-1),
                     jnp.concatenate([pi[L, 0], pi[L, 1]], axis=-1)], axis=1)
    return mvec, w_in.astype(BF16), w_out.astype(BF16), lam


def _ssm_body(u_ref, mvec_ref, win_ref, wout_ref, lam_ref, o_ref, tz_ref, sr_ref, si_ref,
              xr_ref, xi_ref, *, gc, p2, segments):
    L = SSM_CHUNK
    nck = u_ref.shape[1]
    row = lax.broadcasted_iota(jnp.int32, (L, L), 0)
    col = lax.broadcasted_iota(jnp.int32, (L, L), 1)
    upper = col >= row

    def build(i, carry):
        for o in range(gc):
            vec = mvec_ref[0, pl.ds(i * gc + o, 1), :]
            lo = jnp.broadcast_to(vec[:, :L], (L, L))
            hi = jnp.broadcast_to(vec[:, L:], (L, L))
            r_lo = pltpu.roll(lo, 0, 1, stride=1, stride_axis=0)
            r_hi = pltpu.roll(hi, 0, 1, stride=1, stride_axis=0)
            blk = jnp.where(upper, r_hi, r_lo)
            tz_ref[pl.ds(pl.multiple_of(i * L, L), L), o * L:(o + 1) * L] = blk.astype(BF16)
        return carry

    lax.fori_loop(0, gc, build, 0)

    lhs = jnp.concatenate([u_ref[i] for i in range(gc)], axis=-1)
    s = jnp.dot(lhs, win_ref[0], preferred_element_type=F32)
    sr_ref[...] = s[:, 0:p2]
    si_ref[...] = s[:, p2:2 * p2]
    lam_r, lam_i = lam_ref[0, 0:1, :], lam_ref[0, 1:2, :]
    fwd_lane = lax.broadcasted_iota(jnp.int32, (1, p2), 1) < p2 // 2

    for base, nseq, nc in segments:
        cr = jnp.zeros((nseq, p2), F32)
        ci = jnp.zeros((nseq, p2), F32)
        for c in range(nc):
            rows = pl.ds(base + c, nseq, stride=nc) if nseq > 1 else pl.ds(base + c, 1)
            xr_ref[rows, :] = cr
            xi_ref[rows, :] = ci
            sr, si = sr_ref[rows, :], si_ref[rows, :]
            cr, ci = sr + lam_r * cr - lam_i * ci, si + lam_r * ci + lam_i * cr
        cr = jnp.zeros((nseq, p2), F32)
        ci = jnp.zeros((nseq, p2), F32)
        for c in range(nc - 1, -1, -1):
            rows = pl.ds(base + c, nseq, stride=nc) if nseq > 1 else pl.ds(base + c, 1)
            xr_ref[rows, :] = jnp.where(fwd_lane, xr_ref[rows, :], cr)
            xi_ref[rows, :] = jnp.where(fwd_lane, xi_ref[rows, :], ci)
            sr, si = sr_ref[rows, :], si_ref[rows, :]
            cr, ci = sr + lam_r * cr - lam_i * ci, si + lam_r * ci + lam_i * cr

    y = jnp.dot(lhs, tz_ref[...], preferred_element_type=F32)
    x_carry = jnp.concatenate([xr_ref[...], xi_ref[...]], axis=-1).astype(BF16)
    y = y + jnp.dot(x_carry, wout_ref[0], preferred_element_type=F32)
    for o in range(gc):
        o_ref[o] = jax.nn.gelu(y[:, o * L:(o + 1) * L]).astype(o_ref.dtype)


def _ssm(u3, tables, segments):
    mvec, w_in, w_out, lam = tables
    ds, nck, L = u3.shape
    g_n = mvec.shape[0]
    gc = ds // g_n
    p2 = lam.shape[-1]
    body = functools.partial(_ssm_body, gc=gc, p2=p2, segments=segments)
    return pl.pallas_call(
        body,
        grid=(g_n,),
        in_specs=[pl.BlockSpec((gc, nck, L), lambda g: (g, 0, 0)),
                  pl.BlockSpec((1, gc * gc, 2 * L), lambda g: (g, 0, 0)),
                  pl.BlockSpec((1, gc * L, 2 * p2), lambda g: (g, 0, 0)),
                  pl.BlockSpec((1, 2 * p2, gc * L), lambda g: (g, 0, 0)),
                  pl.BlockSpec((1, 2, p2), lambda g: (g, 0, 0))],
        out_specs=pl.BlockSpec((gc, nck, L), lambda g: (g, 0, 0)),
        out_shape=jax.ShapeDtypeStruct((ds, nck, L), BF16),
        scratch_shapes=[pltpu.VMEM((gc * L, gc * L), BF16)] + [pltpu.VMEM((nck, p2), F32)] * 4,
        compiler_params=_params(1),
        name="ssm",
    )(u3, mvec, w_in, w_out, lam)


def _glu_body(x_ref, wa_ref, wb_ref, o_ref):
    x = x_ref[...]
    sw = o_ref.shape[1] // COL_SUBTILES
    for s in range(COL_SUBTILES):
        cols = slice(s * sw, (s + 1) * sw)
        a = jnp.dot(x, wa_ref[:, cols], preferred_element_type=F32)
        b = jnp.dot(x, wb_ref[:, cols], preferred_element_type=F32)
        o_ref[:, cols] = (a * jax.nn.sigmoid(b)).astype(o_ref.dtype)


def _glu(x, w, tm=1024, tn=512):
    m, k = x.shape
    n = w.shape[1] // 2
    tm, tn = min(tm, m), min(tn, n)
    assert m % tm == 0 and n % tn == 0
    nj = n // tn
    return pl.pallas_call(
        _glu_body,
        grid=(m // tm, nj),
        in_specs=[pl.BlockSpec((tm, k), lambda i, j: (i, 0)),
                  pl.BlockSpec((k, tn), lambda i, j: (0, j)),
                  pl.BlockSpec((k, tn), lambda i, j: (0, j + nj))],
        out_specs=pl.BlockSpec((tm, tn), lambda i, j: (i, j)),
        out_shape=jax.ShapeDtypeStruct((m, n), BF16),
        compiler_params=_params(2),
        name="glu",
    )(x, w, w)


def _mix_body(a_ref, w_ref, ga_ref, gb_ref, yb_ref, o_ref):
    a = a_ref[...]
    sw = o_ref.shape[1] // COL_SUBTILES
    for s in range(COL_SUBTILES):
        cols = slice(s * sw, (s + 1) * sw)
        ya = jnp.dot(a, w_ref[:, cols], preferred_element_type=F32)
        o_ref[:, cols] = (ga_ref[:, cols].astype(F32) * ya
                          + gb_ref[:, cols].astype(F32) * yb_ref[:, cols].astype(F32)
                          ).astype(o_ref.dtype)


def _mix(attn, w_ao, gates, y_b, tm=1024, tn=512):
    m, k = attn.shape
    n = w_ao.shape[1]
    tm, tn = min(tm, m), min(tn, n)
    assert m % tm == 0 and n % tn == 0
    ga0, gb0 = 0, n // tn
    return pl.pallas_call(
        _mix_body,
        grid=(m // tm, n // tn),
        in_specs=[pl.BlockSpec((tm, k), lambda i, j: (i, 0)),
                  pl.BlockSpec((k, tn), lambda i, j: (0, j)),
                  pl.BlockSpec((tm, tn), lambda i, j: (i, ga0 + j)),
                  pl.BlockSpec((tm, tn), lambda i, j: (i, gb0 + j)),
                  pl.BlockSpec((tm, tn), lambda i, j: (i, j))],
        out_specs=pl.BlockSpec((tm, tn), lambda i, j: (i, j)),
        out_shape=jax.ShapeDtypeStruct((m, n), BF16),
        compiler_params=_params(2),
        name="mix",
    )(attn, w_ao, gates, gates, y_b)


def _ln1_router_body(mixed_ref, wmix_ref, x_ref, g_ref, b_ref, wr_ref, br_ref, tri_ref,
                     x1_ref, x1p_ref, info_ref, cnt_ref, run_ref, *, alpha, sub):
    i = pl.program_id(0)

    @pl.when(i == 0)
    def _():
        run_ref[...] = jnp.zeros_like(run_ref)

    run = run_ref[...]
    lane = lax.broadcasted_iota(jnp.int32, (sub, LANES), 1).astype(F32)
    for r in range(x_ref.shape[0] // sub):
        rows = slice(r * sub, (r + 1) * sub)
        z = alpha * x_ref[rows, :].astype(F32) + jnp.dot(mixed_ref[rows, :], wmix_ref[...],
                                                          preferred_element_type=F32)
        y = _layer_norm_rows(z, g_ref[...], b_ref[...])
        x1_ref[rows, :] = y
        x1p_ref[rows, :] = _pack_bf16_pairs(y)
        y_hi = y.astype(BF16)
        y_lo = (y - y_hi.astype(F32)).astype(BF16)
        d_hi = jnp.dot(y_hi, wr_ref[...], preferred_element_type=F32)
        d_lo = jnp.dot(y_lo, wr_ref[:, 0:LANES], preferred_element_type=F32)
        logits = d_hi[:, 0:LANES] + d_hi[:, LANES:2 * LANES] + d_lo + br_ref[...]
        rem = logits
        idxs, vals = [], []
        for _ in range(TOP_K):
            mx = jnp.max(rem, axis=-1, keepdims=True)
            idx = jnp.min(jnp.where(rem == mx, lane, float(LANES)), axis=-1, keepdims=True)
            idxs.append(idx)
            vals.append(mx)
            rem = jnp.where(lane == idx, -jnp.inf, rem)
        exps = [jnp.exp(v - vals[0]) for v in vals]
        rden = 1.0 / (exps[0] + exps[1] + exps[2] + exps[3])
        onehot = jnp.zeros((sub, LANES), F32)
        for idx in idxs:
            onehot = onehot + jnp.where(lane == idx, 1.0, 0.0)
        prefix = jnp.dot(tri_ref[...], onehot.astype(BF16), preferred_element_type=F32) + run
        info = jnp.zeros((sub, LANES), F32)
        for k in range(TOP_K):
            rank = jnp.sum(jnp.where(lane == idxs[k], prefix, 0.0), axis=-1, keepdims=True)
            info = info + jnp.where(lane == float(k), idxs[k], 0.0)
            info = info + jnp.where(lane == float(TOP_K + k), exps[k] * rden, 0.0)
            info = info + jnp.where(lane == float(2 * TOP_K + k), rank, 0.0)
        info_ref[rows, :] = info
        run = run + jnp.sum(onehot, axis=0, keepdims=True)
    run_ref[...] = run
    cnt_ref[...] = run


def _ln1_router(mixed, w_mix, x, ln_g, ln_b, w_r, b_r, alpha, tm=512, sub=256):
    t, d = mixed.shape
    e = w_r.shape[1]
    tm = min(tm, t)
    sub = min(sub, tm)
    assert t % tm == 0 and tm % sub == 0 and e <= LANES
    wr_pad = jnp.zeros((d, LANES), F32).at[:, :e].set(w_r.astype(F32))
    wr_hi = wr_pad.astype(BF16)
    wr_lo = (wr_pad - wr_hi.astype(F32)).astype(BF16)
    wr_split = jnp.concatenate([wr_hi, wr_lo], axis=1)
    br_pad = jnp.full((1, LANES), MASK_VALUE, F32).at[0, :e].set(b_r.astype(F32))
    tri = jnp.tril(jnp.ones((sub, sub), BF16), -1)
    row = lambda v: v.astype(F32).reshape(1, d)
    const = lambda i: (0, 0)
    return pl.pallas_call(
        functools.partial(_ln1_router_body, alpha=alpha, sub=sub),
        grid=(t // tm,),
        in_specs=[pl.BlockSpec((tm, d), lambda i: (i, 0)),
                  pl.BlockSpec((d, d), const),
                  pl.BlockSpec((tm, d), lambda i: (i, 0)),
                  pl.BlockSpec((1, d), const), pl.BlockSpec((1, d), const),
                  pl.BlockSpec((d, 2 * LANES), const), pl.BlockSpec((1, LANES), const),
                  pl.BlockSpec((sub, sub), const)],
        out_specs=[pl.BlockSpec((tm, d), lambda i: (i, 0)),
                   pl.BlockSpec((tm, d // 2), lambda i: (i, 0)),
                   pl.BlockSpec((tm, LANES), lambda i: (i, 0)),
                   pl.BlockSpec((1, LANES), const)],
        out_shape=[jax.ShapeDtypeStruct((t, d), F32),
                   jax.ShapeDtypeStruct((t, d // 2), jnp.uint32),
                   jax.ShapeDtypeStruct((t, LANES), F32),
                   jax.ShapeDtypeStruct((1, LANES), F32)],
        scratch_shapes=[pltpu.VMEM((1, LANES), F32)],
        compiler_params=_params(1),
        name="ln1_router",
    )(mixed, w_mix, x, row(ln_g), row(ln_b), wr_split, br_pad, tri)


def _row_copy(src, src_row, dst, dst_row, sem):
    return pltpu.make_async_copy(src.at[pl.ds(src_row, 1), :], dst.at[pl.ds(dst_row, 1), :], sem)


def _dispatch_body(pos_ref, x_ref, xs_in_hbm, xs_hbm, pos_smem, sem_idx, sem_rows, *, tm):
    del xs_in_hbm
    idx_cp = pltpu.make_async_copy(pos_ref.at[0, 0], pos_smem, sem_idx)
    idx_cp.start()
    idx_cp.wait()

    def issue(r, carry):
        for k in range(TOP_K):
            _row_copy(x_ref, r, xs_hbm, pos_smem[k * tm + r], sem_rows).start(priority=k % 2)
        return carry

    lax.fori_loop(0, tm, issue, 0)
    for k in range(TOP_K):
        pltpu.make_async_copy(x_ref, xs_hbm.at[pl.ds(0, tm), :], sem_rows).wait()


def _dispatch(x1, pos_tiles, xs_buf, tm):
    t, d = x1.shape
    return pl.pallas_call(
        functools.partial(_dispatch_body, tm=tm),
        grid=(t // tm,),
        in_specs=[pl.BlockSpec((1, 1, TOP_K * tm), lambda i: (i, 0, 0)),
                  pl.BlockSpec((tm, d), lambda i: (i, 0)),
                  pl.BlockSpec(memory_space=pl.ANY)],
        out_specs=pl.BlockSpec(memory_space=pl.ANY),
        out_shape=jax.ShapeDtypeStruct(xs_buf.shape, xs_buf.dtype),
        input_output_aliases={2: 0},
        scratch_shapes=[pltpu.SMEM((TOP_K * tm,), jnp.int32),
                        pltpu.SemaphoreType.DMA(()), pltpu.SemaphoreType.DMA(())],
        compiler_params=_params(1),
        name="moe_dispatch",
    )(pos_tiles, x1, xs_buf)


def _expert_body(be_ref, nv_ref, xs_ref, w1g_ref, w1u_ref, b1g_ref, b1u_ref, w2_ref, b2_ref, o_ref,
                 acc_ref, *, sub):
    b = pl.program_id(0)
    f = pl.program_id(1)
    last = pl.num_programs(1) - 1
    nv = nv_ref[b]
    for r in range(xs_ref.shape[0] // sub):
        rows = slice(r * sub, (r + 1) * sub)

        @pl.when(jnp.logical_and(nv > r * sub, f == 0))
        def _():
            acc_ref[rows, :] = jnp.broadcast_to(b2_ref[0], (sub, acc_ref.shape[1]))

        @pl.when(nv > r * sub)
        def _():
            row = lax.broadcasted_iota(jnp.int32, (sub, 1), 0) + r * sub
            xa, xb = _unpack_bf16_pairs(jnp.where(row < nv, xs_ref[rows, :], jnp.uint32(0)))
            x = jnp.concatenate([xa, xb], axis=1).astype(BF16)
            hg = jnp.dot(x, w1g_ref[0], preferred_element_type=F32) + b1g_ref[0]
            hu = jnp.dot(x, w1u_ref[0], preferred_element_type=F32) + b1u_ref[0]
            gate = jnp.minimum(hg, SWIGLU_LIMIT)
            up = jnp.clip(hu, -SWIGLU_LIMIT, SWIGLU_LIMIT)
            act = (up + 1.0) * (gate * jax.nn.sigmoid(SWIGLU_ALPHA * gate))
            y = acc_ref[rows, :] + jnp.dot(act.astype(BF16), w2_ref[0], preferred_element_type=F32)

            @pl.when(f < last)
            def _():
                acc_ref[rows, :] = y

            @pl.when(f == last)
            def _():
                o_ref[rows, :] = _pack_bf16_pairs(y)

        @pl.when(jnp.logical_and(nv <= r * sub, f == last))
        def _():
            o_ref[rows, :] = jnp.zeros((sub, o_ref.shape[1]), o_ref.dtype)


def _experts(xs, block_e, block_nv, w1, b1, w2, b2, tm, tf=1024, sub=512):
    n_slots, dh = xs.shape
    d = 2 * dh
    e, ff = w2.shape[0], w2.shape[1]
    tf = min(tf, ff)
    sub = min(sub, tm)
    nf = ff // tf
    assert ff % tf == 0 and n_slots % tm == 0 and tm % sub == 0
    b1r = b1.astype(F32).reshape(e, 1, 2 * ff)
    b2r = b2.astype(F32).reshape(e, 1, d)
    fi = lambda b, f, nv: jnp.where(nv[b] > 0, f, nf - 1)
    grid_spec = pltpu.PrefetchScalarGridSpec(
        num_scalar_prefetch=2,
        grid=(n_slots // tm, nf),
        in_specs=[pl.BlockSpec((tm, dh), lambda b, f, be, nv: (b, 0)),
                  pl.BlockSpec((1, d, tf), lambda b, f, be, nv: (be[b], 0, fi(b, f, nv))),
                  pl.BlockSpec((1, d, tf), lambda b, f, be, nv: (be[b], 0, nf + fi(b, f, nv))),
                  pl.BlockSpec((1, 1, tf), lambda b, f, be, nv: (be[b], 0, fi(b, f, nv))),
                  pl.BlockSpec((1, 1, tf), lambda b, f, be, nv: (be[b], 0, nf + fi(b, f, nv))),
                  pl.BlockSpec((1, tf, d), lambda b, f, be, nv: (be[b], fi(b, f, nv), 0)),
                  pl.BlockSpec((1, 1, d), lambda b, f, be, nv: (be[b], 0, 0))],
        out_specs=pl.BlockSpec((tm, dh), lambda b, f, be, nv: (b, 0)),
        scratch_shapes=[pltpu.VMEM((tm, d), F32)],
    )
    return pl.pallas_call(
        functools.partial(_expert_body, sub=sub),
        grid_spec=grid_spec,
        out_shape=jax.ShapeDtypeStruct((n_slots, dh), jnp.uint32),
        compiler_params=_params(2),
        name="moe_experts",
    )(block_e, block_nv, xs, w1, w1, b1r, b1r, w2, b2r)


def _combine_body(pos_ref, pos_next_ref, info_ref, x1_ref, g_ref, b_ref, ys_hbm, x2_ref, x2b_ref,
                  pos_smem, buf, sem_idx, sem_rows, *, tm, alpha):
    i = pl.program_id(0)
    n = pl.num_programs(0)
    slot = i % 2

    def start_gather(pos_vmem, dst_slot):
        idx_cp = pltpu.make_async_copy(pos_vmem.at[0, 0], pos_smem, sem_idx)
        idx_cp.start()
        idx_cp.wait()

        def issue(r8, carry):
            r0 = pl.multiple_of(r8 * SUBLANES, SUBLANES)
            for j in range(SUBLANES):
                for k in range(TOP_K):
                    _row_copy(ys_hbm, pos_smem[k * tm + r0 + j], buf.at[dst_slot], k * tm + r0 + j,
                              sem_rows.at[dst_slot]).start(priority=k % 2)
            return carry

        lax.fori_loop(0, tm // SUBLANES, issue, 0)

    @pl.when(i == 0)
    def _():
        start_gather(pos_ref, 0)

    @pl.when(i + 1 < n)
    def _():
        start_gather(pos_next_ref, 1 - slot)

    pltpu.make_async_copy(ys_hbm.at[pl.ds(0, TOP_K * tm), :], buf.at[slot], sem_rows.at[slot]).wait()

    info = info_ref[...]
    dh = buf.shape[2]
    fa = jnp.zeros((tm, dh), F32)
    fb = jnp.zeros((tm, dh), F32)
    for k in range(TOP_K):
        gate = info[:, TOP_K + k:TOP_K + k + 1]
        ya, yb = _unpack_bf16_pairs(buf[slot, k * tm:(k + 1) * tm, :])
        fa = fa + gate * ya
        fb = fb + gate * yb
    f = jnp.concatenate([fa, fb], axis=1)
    y = _layer_norm_rows(alpha * x1_ref[...] + f, g_ref[...], b_ref[...])
    x2_ref[...] = y
    x2b_ref[...] = y.astype(BF16)


def _combine_ln2(ys, pos_tiles, info, x1, ln_g, ln_b, alpha, tm):
    t, d = x1.shape
    nt = t // tm
    row = lambda v: v.astype(F32).reshape(1, d)
    const = lambda i: (0, 0)
    return pl.pallas_call(
        functools.partial(_combine_body, tm=tm, alpha=alpha),
        grid=(nt,),
        in_specs=[pl.BlockSpec((1, 1, TOP_K * tm), lambda i: (i, 0, 0)),
                  pl.BlockSpec((1, 1, TOP_K * tm), lambda i: (jnp.minimum(i + 1, nt - 1), 0, 0)),
                  pl.BlockSpec((tm, LANES), lambda i: (i, 0)),
                  pl.BlockSpec((tm, d), lambda i: (i, 0)),
                  pl.BlockSpec((1, d), const), pl.BlockSpec((1, d), const),
                  pl.BlockSpec(memory_space=pl.ANY)],
        out_specs=[pl.BlockSpec((tm, d), lambda i: (i, 0)),
                   pl.BlockSpec((tm, d), lambda i: (i, 0))],
        out_shape=[jax.ShapeDtypeStruct((t, d), F32), jax.ShapeDtypeStruct((t, d), BF16)],
        scratch_shapes=[pltpu.SMEM((TOP_K * tm,), jnp.int32),
                        pltpu.VMEM((2, TOP_K * tm, d // 2), jnp.uint32),
                        pltpu.SemaphoreType.DMA(()), pltpu.SemaphoreType.DMA((2,))],
        compiler_params=_params(1),
        name="moe_combine_ln2",
    )(pos_tiles, pos_tiles, info, x1, row(ln_g), row(ln_b), ys)


def _pos_tiles(pos, tm):
    t = pos.shape[0]
    return pos.reshape(t // tm, tm, TOP_K).transpose(0, 2, 1).reshape(t // tm, 1, TOP_K * tm)


def _route(info, counts, n_exp, block):
    t = info.shape[0]
    idx = info[:, 0:TOP_K].astype(jnp.int32)
    rank = info[:, 2 * TOP_K:3 * TOP_K].astype(jnp.int32)
    cnt = counts[0, :n_exp].astype(jnp.int32)
    padded = (cnt + block - 1) // block * block
    pends = jnp.cumsum(padded)
    pstarts = pends - padded
    experts = jnp.arange(n_exp, dtype=jnp.int32)
    table = lambda tab, ids: jnp.sum(jnp.where(ids[..., None] == experts, tab, 0), axis=-1)
    pos = table(pstarts, idx) + rank
    n_blocks = _num_slot_blocks(t, n_exp, block)
    bstart = jnp.arange(n_blocks, dtype=jnp.int32) * block
    block_e = jnp.minimum(jnp.sum((pends[None, :] <= bstart[:, None]).astype(jnp.int32), axis=1),
                          n_exp - 1)
    block_nv = jnp.clip(table(pstarts + cnt, block_e) - bstart, 0, block).astype(jnp.int32)
    return pos, block_e, block_nv


def _num_slot_blocks(t, n_exp, block):
    return (t * TOP_K) // block + n_exp


def _encoder_layer(x, xb, xs_buf, geom, w_in, sink, w_ao, ssm_tabs, w_glu, w_mix, ln1_g, ln1_b,
                   w_r, b_r, w1, b1, w2, b2, ln2_g, ln2_b, *, layer, alpha, dims, slot_block,
                   tok_tile):
    n_heads, n_kv, hd, d_ssm, d = dims
    t = x.shape[0]
    q_w, kv_w = n_heads * hd, n_kv * hd
    u_col0 = q_w + 2 * kv_w
    gate_col0 = u_col0 + d_ssm
    attn_geom, ssm_segments = geom

    proj = _in_proj(xb, w_in, 0, gate_col0, gate=False)
    gates = _in_proj(xb, w_in, gate_col0, 2 * d, gate=True)
    attn = _attention(proj, sink, _attn_bias(n_heads), n_heads, n_kv, hd, attn_geom)
    u3 = proj[:, u_col0:gate_col0].T.reshape(d_ssm, t // SSM_CHUNK, SSM_CHUNK)
    gy = _ssm(u3, ssm_tabs, ssm_segments).reshape(d_ssm, t).T
    y_b = _glu(gy, w_glu)
    mixed = _mix(attn, w_ao, gates, y_b)
    x1, x1p, info, counts = _ln1_router(mixed, w_mix, x, ln1_g, ln1_b, w_r, b_r, alpha)

    n_exp = w_r.shape[1]
    pos, block_e, block_nv = _route(info, counts, n_exp, slot_block)
    pos_t = _pos_tiles(pos, tok_tile)
    xs = _dispatch(x1p, pos_t, xs_buf, tok_tile)
    ys = _experts(xs, block_e + layer * n_exp, block_nv, w1, b1, w2, b2, slot_block)
    x2, x2b = _combine_ln2(ys, pos_t, info, x1, ln2_g, ln2_b, alpha, tok_tile)
    return x2, x2b, xs


def kernel(x_prompt, x_sample, w_in, attn_sink, w_attn_out, ssm_a_re, ssm_a_im, ssm_log_dt, ssm_b_re, ssm_b_im, ssm_c_re, ssm_c_im, ssm_d, w_glu, w_mix_out, ln1_g, ln1_b, router_w, router_b, expert_w1, expert_b1, expert_w2, expert_b2, ln2_g, ln2_b):
    depth = w_in.shape[0]
    bp, sp, d = x_prompt.shape
    bs, ss, _ = x_sample.shape
    n_heads = attn_sink.shape[1]
    hd = d // n_heads
    d_ssm = ssm_a_re.shape[2] * ssm_b_re.shape[4]
    kv_w = (w_in.shape[2] - d - d_ssm - 2 * d) // 2
    dims = (n_heads, kv_w // hd, hd, d_ssm, d)
    alpha = (2.0 * depth) ** 0.25
    assert sp % ATTN_BLOCK == 0 and ss % ATTN_BLOCK == 0 and ATTN_BLOCK == SSM_CHUNK

    tp = bp * sp
    attn_geom = (tp // ATTN_BLOCK, sp // ATTN_BLOCK, ss // ATTN_BLOCK)
    ssm_segments = ((0, bp, sp // SSM_CHUNK), (tp // SSM_CHUNK, bs, ss // SSM_CHUNK))
    geom = (attn_geom, ssm_segments)

    x = jnp.concatenate([x_prompt.reshape(tp, d), x_sample.reshape(bs * ss, d)], axis=0)
    xb = x
    t = x.shape[0]
    slot_block, tok_tile = min(SLOT_BLOCK, t), min(TOKEN_TILE, t)
    n_slots = _num_slot_blocks(t, router_w.shape[2], slot_block) * slot_block
    xs_buf = jnp.zeros((n_slots, d // 2), jnp.uint32)
    col_scale = jnp.where(jnp.arange(w_in.shape[2]) < n_heads * hd, hd ** -0.5 * LOG2E, 1.0)
    w_in_b = (w_in * col_scale.astype(F32)).astype(BF16)
    n_exp, ff = expert_w2.shape[1], expert_w2.shape[2]
    w1_all = expert_w1.astype(BF16).reshape(depth * n_exp, d, 2 * ff)
    w2_all = expert_w2.astype(BF16).reshape(depth * n_exp, ff, d)
    b1_all = expert_b1.reshape(depth * n_exp, 2 * ff)
    b2_all = expert_b2.reshape(depth * n_exp, d)
    for l in range(depth):
        tabs = _ssm_tables(ssm_a_re[l], ssm_a_im[l], ssm_log_dt[l], ssm_b_re[l], ssm_b_im[l],
                           ssm_c_re[l], ssm_c_im[l], ssm_d[l])
        x, xb, xs_buf = _encoder_layer(
            x, xb, xs_buf, geom, w_in_b[l], attn_sink[l].astype(F32),
            w_attn_out[l].astype(BF16), tabs, w_glu[l].astype(BF16), w_mix_out[l].astype(BF16),
            ln1_g[l], ln1_b[l], router_w[l], router_b[l], w1_all, b1_all, w2_all, b2_all,
            ln2_g[l], ln2_b[l], layer=l, alpha=alpha, dims=dims,
            slot_block=slot_block, tok_tile=tok_tile)
    return (x[:tp].reshape(bp, sp, d), x[tp:].reshape(bs, ss, d))
```
